```python
import jax, jax.numpy as jnp
from jax import lax
import numpy as np

D_MODEL = 1024
BATCH = 16
SEQ = 4096
DEPTH = 2

CHUNK = 64
CONV_CH = 512
CONV_WIDTH = 31
ATT_HEADS = 8
HEAD_DIM = 64
ATT_WIDTH = ATT_HEADS * HEAD_DIM
IDX_HEADS = 8
IDX_DIM = 64
TOPK_MAX = 256
Q_BLOCK = CHUNK
MIX_WIDTH = CONV_CH + ATT_WIDTH
IN0_WIDTH = 2 * CONV_CH + 3 * ATT_WIDTH + IDX_HEADS * IDX_DIM + IDX_DIM + IDX_HEADS
SC_WIDTH = 1024
SC_CONV_WIDTH = 3
D_FF = 2816
N_EXPERTS = 8
TOP_K_EXPERTS = 2
D_FF_EXPERT = 2816
N_EVEN = (DEPTH + 1) // 2
N_ODD = DEPTH // 2
EPS = 1e-6

kernel_name = "chunk_causal_conformer_dsa_shortconv_moe_adaln"


def _split(t, sizes):
    out, o = [], 0
    for n in sizes:
        out.append(t[..., o:o + n])
        o += n
    return out


def rmsnorm(x, g):
    xf = x.astype(jnp.float32)
    y = xf * lax.rsqrt(jnp.mean(xf * xf, axis=-1, keepdims=True) + EPS)
    return (y * g.astype(jnp.float32)).astype(x.dtype)


def layernorm(x, g, b):
    xf = x.astype(jnp.float32)
    mu = jnp.mean(xf, axis=-1, keepdims=True)
    var = jnp.mean(jnp.square(xf - mu), axis=-1, keepdims=True)
    y = (xf - mu) * lax.rsqrt(var + EPS)
    return (y * g.astype(jnp.float32) + b.astype(jnp.float32)).astype(x.dtype)


def causal_depthwise_conv(x, w):
    W, C = w.shape
    return lax.conv_general_dilated(
        x, w[:, None, :], window_strides=(1,), padding=[(W - 1, 0)],
        dimension_numbers=("NWC", "WIO", "NWC"), feature_group_count=C)


def adaln(c, w_ada, b_ada):
    mod = jax.nn.silu(c) @ w_ada + b_ada
    return _split(mod[:, None, :], [D_MODEL] * 6)


def swiglu(h, w_gate, w_up, w_down):
    return (jax.nn.silu(h @ w_gate) * (h @ w_up)) @ w_down


def dsa_attention(q, k, v, qi, ki, wi):
    B_, S, H, Dh = q.shape
    topk = min(TOPK_MAX, S // 4)
    nblk = S // Q_BLOCK
    key_pos = jnp.arange(S)
    gather = jax.vmap(lambda t, idx: t[idx])

    def to_blocks(t):
        return t.reshape(B_, nblk, Q_BLOCK, *t.shape[2:]).swapaxes(0, 1)

    def block(args):
        blk, qb, qib, wib = args
        q_pos = blk * Q_BLOCK + jnp.arange(Q_BLOCK)
        limit = (q_pos // CHUNK + 1) * CHUNK
        admissible = key_pos[None, :] < limit[:, None]
        dots = jnp.einsum("bqhd,bsd->bqhs", qib, ki).astype(jnp.float32)
        score = jnp.einsum("bqh,bqhs->bqs", wib.astype(jnp.float32), jax.nn.relu(dots))
        score = jnp.where(admissible[None], score, -jnp.inf)
        _, idx = lax.top_k(score, topk)
        valid = idx < limit[None, :, None]
        k_sel = gather(k, idx)
        v_sel = gather(v, idx)
        logits = jnp.einsum("bqhd,bqkhd->bhqk", qb, k_sel).astype(jnp.float32) * (Dh ** -0.5)
        logits = jnp.where(valid[:, None], logits, -jnp.inf)
        p = jax.nn.softmax(logits, axis=-1).astype(v.dtype)
        return jnp.einsum("bhqk,bqkhd->bqhd", p, v_sel)

    out = lax.map(block, (jnp.arange(nblk), to_blocks(q), to_blocks(qi), to_blocks(wi)))
    return out.swapaxes(0, 1).reshape(B_, S, H, Dh)


def conv_attn_mixer(h, w_in, conv_w, conv_b, cn_g, cn_b, q_g, k_g, w_out):
    B_, S, _ = h.shape
    proj = h @ w_in
    u, q, k, v, qi, ki, wi = _split(
        proj, [2 * CONV_CH, ATT_WIDTH, ATT_WIDTH, ATT_WIDTH, IDX_HEADS * IDX_DIM, IDX_DIM, IDX_HEADS])
    a, g = _split(u, [CONV_CH, CONV_CH])
    a = a * jax.nn.sigmoid(g)
    a = causal_depthwise_conv(a, conv_w) + conv_b
    a = jax.nn.silu(layernorm(a, cn_g, cn_b))
    q = rmsnorm(q.reshape(B_, S, ATT_HEADS, HEAD_DIM), q_g)
    k = rmsnorm(k.reshape(B_, S, ATT_HEADS, HEAD_DIM), k_g)
    v = v.reshape(B_, S, ATT_HEADS, HEAD_DIM)
    qi = qi.reshape(B_, S, IDX_HEADS, IDX_DIM)
    wi = wi * ((IDX_DIM * IDX_HEADS) ** -0.5)
    att = dsa_attention(q, k, v, qi, ki, wi).reshape(B_, S, ATT_WIDTH)
    return jnp.concatenate([a, att], axis=-1) @ w_out


def short_conv_mixer(h, w_in, conv_w, w_out):
    bg, cg, v = _split(h @ w_in, [SC_WIDTH] * 3)
    return (bg * causal_depthwise_conv(cg * v, conv_w)) @ w_out


def moe_swiglu(h, w_router, w_gate, w_up, w_down):
    B_, S, D = h.shape
    t = h.reshape(-1, D)
    logits = (t @ w_router).astype(jnp.float32)
    top_val, top_idx = lax.top_k(logits, TOP_K_EXPERTS)
    top_w = jax.nn.softmax(top_val, axis=-1)
    gates = jnp.sum(jax.nn.one_hot(top_idx, N_EXPERTS, dtype=jnp.float32) * top_w[..., None], axis=1)
    gates = gates.astype(t.dtype)
    out = jnp.zeros_like(t)
    for e in range(N_EXPERTS):
        out = out + gates[:, e:e + 1] * swiglu(t, w_gate[e], w_up[e], w_down[e])
    return out.reshape(B_, S, D)


def setup_inputs(seed: int = 0) -> dict:
    key = jax.random.key(seed)
    ks = iter(jax.random.split(key, 32))
    f32 = jnp.float32

    def nrm(shape, scale):
        return jax.random.normal(next(ks), shape, f32) * scale

    D = D_MODEL
    return {
        "x": nrm((BATCH, SEQ, D), 1.0),
        "c": nrm((BATCH, D), 1.0),
        "ada_w": nrm((DEPTH, D, 6 * D), 0.5 * D ** -0.5),
        "ada_b": nrm((DEPTH, 6 * D), 0.02),
        "norm_g": 1.0 + nrm((DEPTH, 2, D), 0.1),
        "ab_w_in": nrm((N_EVEN, D, IN0_WIDTH), D ** -0.5),
        "ab_conv_w": nrm((N_EVEN, CONV_WIDTH, CONV_CH), CONV_WIDTH ** -0.5),
        "ab_conv_b": nrm((N_EVEN, CONV_CH), 0.02),
        "ab_cnorm_g": 1.0 + nrm((N_EVEN, CONV_CH), 0.1),
        "ab_cnorm_b": nrm((N_EVEN, CONV_CH), 0.02),
        "ab_q_g": 1.0 + nrm((N_EVEN, HEAD_DIM), 0.1),
        "ab_k_g": 1.0 + nrm((N_EVEN, HEAD_DIM), 0.1),
        "ab_w_out": nrm((N_EVEN, MIX_WIDTH, D), MIX_WIDTH ** -0.5),
        "ffn_w_gate": nrm((N_EVEN, D, D_FF), D ** -0.5),
        "ffn_w_up": nrm((N_EVEN, D, D_FF), D ** -0.5),
        "ffn_w_down": nrm((N_EVEN, D_FF, D), D_FF ** -0.5),
        "sc_w_in": nrm((N_ODD, D, 3 * SC_WIDTH), D ** -0.5),
        "sc_conv_w": nrm((N_ODD, SC_CONV_WIDTH, SC_WIDTH), SC_CONV_WIDTH ** -0.5),
        "sc_w_out": nrm((N_ODD, SC_WIDTH, D), SC_WIDTH ** -0.5),
        "moe_router": nrm((N_ODD, D, N_EXPERTS), D ** -0.5),
        "moe_w_gate": nrm((N_ODD, N_EXPERTS, D, D_FF_EXPERT), D ** -0.5),
        "moe_w_up": nrm((N_ODD, N_EXPERTS, D, D_FF_EXPERT), D ** -0.5),
        "moe_w_down": nrm((N_ODD, N_EXPERTS, D_FF_EXPERT, D), D_FF_EXPERT ** -0.5),
    }


def reference(x, c, ada_w, ada_b, norm_g, ab_w_in, ab_conv_w, ab_conv_b, ab_cnorm_g,
              ab_cnorm_b, ab_q_g, ab_k_g, ab_w_out, ffn_w_gate, ffn_w_up, ffn_w_down,
              sc_w_in, sc_conv_w, sc_w_out, moe_router, moe_w_gate, moe_w_up, moe_w_down):
    for i in range(DEPTH):
        shift1, scale1, gate1, shift2, scale2, gate2 = adaln(c, ada_w[i], ada_b[i])
        h = rmsnorm(x, norm_g[i, 0]) * (1.0 + scale1) + shift1
        if i % 2 == 0:
            j = i // 2
            mix = conv_attn_mixer(h, ab_w_in[j], ab_conv_w[j], ab_conv_b[j], ab_cnorm_g[j],
                                  ab_cnorm_b[j], ab_q_g[j], ab_k_g[j], ab_w_out[j])
            x = x + gate1 * mix
            h = rmsnorm(x, norm_g[i, 1]) * (1.0 + scale2) + shift2
            x = x + gate2 * swiglu(h, ffn_w_gate[j], ffn_w_up[j], ffn_w_down[j])
        else:
            j = i // 2
            mix = short_conv_mixer(h, sc_w_in[j], sc_conv_w[j], sc_w_out[j])
            x = x + gate1 * mix
            h = rmsnorm(x, norm_g[i, 1]) * (1.0 + scale2) + shift2
            x = x + gate2 * moe_swiglu(h, moe_router[j], moe_w_gate[j], moe_w_up[j], moe_w_down[j])
    return x
```

```python
import functools

import jax
import jax.numpy as jnp
from jax import lax
from jax.experimental import pallas as pl
from jax.experimental.pallas import tpu as pltpu

D_MODEL = 1024
CHUNK = 64
CONV_CH = 512
CONV_WIDTH = 31
ATT_HEADS = 8
HEAD_DIM = 64
ATT_WIDTH = ATT_HEADS * HEAD_DIM
IDX_HEADS = 8
IDX_DIM = 64
TOPK_MAX = 256
SC_WIDTH = 1024
D_FF = 2816
N_EXPERTS = 8
EPS = 1e-6

LANES = 128
CONV_HALO = 32
SC_HALO = 8
NEG_BIG = -1e30
INT_MIN = -2 ** 31
VMEM_LIMIT = 56 * 1024 * 1024

BF16 = jnp.bfloat16
F32 = jnp.float32


def _dot(a, b):
    return jnp.dot(a, b, preferred_element_type=F32)


def _dot_nt(a, b):
    return lax.dot_general(a, b, (((1,), (1,)), ((), ())), preferred_element_type=F32)


def _split_bf16(a):
    hi = a.astype(BF16)
    lo = (a - hi.astype(F32)).astype(BF16)
    return hi, lo


def _dot3(a, b):
    a_hi, a_lo = _split_bf16(a)
    b_hi, b_lo = _split_bf16(b)
    return _dot(a_hi, b_hi) + (_dot(a_hi, b_lo) + _dot(a_lo, b_hi))


def _silu(x):
    return x * jax.nn.sigmoid(x)


def _norm_mod(x, g, scale, shift):
    ms = jnp.mean(x * x, axis=-1, keepdims=True)
    return (x * lax.rsqrt(ms + EPS) * g) * (1.0 + scale) + shift


def _params(sem, vmem=VMEM_LIMIT):
    return pltpu.CompilerParams(dimension_semantics=sem, vmem_limit_bytes=vmem)


def _resident(shape):
    nd = len(shape)
    return pl.BlockSpec(shape, lambda *_: (0,) * nd, pipeline_mode=pl.Buffered(1))


def _ada_kernel(c_ref, w_ref, b_ref, o_ref):
    o_ref[...] = _dot3(_silu(c_ref[...]), w_ref[...]) + b_ref[...]


def _adaln(c, ada_w, ada_b):
    depth, d, n = ada_w.shape
    b = c.shape[0]
    tn = 1536
    mod = pl.pallas_call(
        _ada_kernel,
        grid=(depth, n // tn),
        in_specs=[
            pl.BlockSpec((b, d), lambda l, j: (0, 0)),
            pl.BlockSpec((None, d, tn), lambda l, j: (l, 0, j)),
            pl.BlockSpec((None, 1, tn), lambda l, j: (l, 0, j)),
        ],
        out_specs=pl.BlockSpec((None, b, tn), lambda l, j: (l, 0, j)),
        out_shape=jax.ShapeDtypeStruct((depth, b, n), F32),
        compiler_params=_params(("arbitrary", "arbitrary")),
        name="adaln",
    )(c, ada_w, ada_b.reshape(depth, 1, n))
    return mod.reshape(depth, b, 6, d)


def _mod_spec(layer, rows_per_batch_tiles):
    return pl.BlockSpec((None, None, 6, D_MODEL),
                        lambda i, *_: (layer, i // rows_per_batch_tiles, 0, 0))


def _in0_kernel(x_ref, mod_ref, g_ref, wm_ref, ws_ref, gm_ref, qg_ref, kg_ref,
                a_ref, q_ref, k_ref, v_ref, qi_ref, sm_ref):
    mod = mod_ref[...]
    h = _norm_mod(x_ref[...], g_ref[...], mod[1:2], mod[0:1]).astype(BF16)
    c = CONV_CH
    u = _dot(h, wm_ref[:, 0:2 * c])
    a_ref[...] = u[:, :c] * jax.nn.sigmoid(u[:, c:])

    def head_norm(t, g):
        ss = _dot((t * t).astype(BF16), gm_ref[...])
        return t * lax.rsqrt(ss * (1.0 / HEAD_DIM) + EPS) * g

    o = 2 * c
    w = ATT_WIDTH
    q_ref[...] = head_norm(_dot(h, wm_ref[:, o:o + w]), qg_ref[...]).astype(BF16)
    k_ref[...] = head_norm(_dot(h, wm_ref[:, o + w:o + 2 * w]), kg_ref[...]).astype(BF16)
    v_ref[...] = _dot(h, wm_ref[:, o + 2 * w:o + 3 * w]).astype(BF16)
    qi_ref[...] = _dot(h, wm_ref[:, o + 3 * w:o + 4 * w]).astype(BF16)
    sm = _dot(h, ws_ref[...])
    lane = lax.broadcasted_iota(jnp.int32, sm.shape, 1)
    is_wi = (lane >= IDX_DIM) & (lane < IDX_DIM + IDX_HEADS)
    sm_ref[...] = jnp.where(is_wi, sm * ((IDX_DIM * IDX_HEADS) ** -0.5), sm)


def _in_proj0(x2, mod, g, w_in, q_g, k_g, seq, tm):
    m, d = x2.shape
    nmain = 2 * CONV_CH + 3 * ATT_WIDTH + IDX_HEADS * IDX_DIM
    nsmall = IDX_DIM + IDX_HEADS
    wm = w_in[:, :nmain].astype(BF16)
    ws = jnp.pad(w_in[:, nmain:nmain + nsmall], ((0, 0), (0, LANES - nsmall))).astype(BF16)
    hid = jnp.arange(ATT_WIDTH) // HEAD_DIM
    gm = (hid[:, None] == hid[None, :]).astype(BF16)
    qg = (jnp.tile(q_g, ATT_HEADS) * (HEAD_DIM ** -0.5))[None]
    kg = jnp.tile(k_g, ATT_HEADS)[None]
    row = lambda n: pl.BlockSpec((tm, n), lambda i: (i, 0))
    return pl.pallas_call(
        _in0_kernel,
        grid=(m // tm,),
        in_specs=[row(d), _mod_spec(0, seq // tm), _resident((1, d)), _resident(wm.shape),
                  _resident(ws.shape), _resident(gm.shape), _resident(qg.shape), _resident(kg.shape)],
        out_specs=[row(CONV_CH), row(ATT_WIDTH), row(ATT_WIDTH), row(ATT_WIDTH), row(ATT_WIDTH), row(LANES)],
        out_shape=[jax.ShapeDtypeStruct((m, CONV_CH), F32)]
        + [jax.ShapeDtypeStruct((m, ATT_WIDTH), BF16)] * 4
        + [jax.ShapeDtypeStruct((m, LANES), F32)],
        compiler_params=_params(("parallel",)),
        name="in_proj0",
    )(x2, mod, g[None], wm, ws, gm, qg, kg)


def _conv_kernel(prev_ref, cur_ref, w_ref, b_ref, g_ref, beta_ref, o_ref, buf, *, ts, rc):
    j = pl.program_id(1)
    buf[0:CONV_HALO, :] = jnp.where(j > 0, prev_ref[...], 0.0)
    buf[CONV_HALO:, :] = cur_ref[...]
    w = w_ref[...]
    lead = CONV_HALO - (CONV_WIDTH - 1)
    for r in range(ts // rc):
        acc = jnp.broadcast_to(b_ref[...], (rc, CONV_CH))
        for k in range(CONV_WIDTH):
            s = r * rc + k + lead
            acc = acc + w[k:k + 1, :] * buf[s:s + rc, :]
        mu = jnp.mean(acc, axis=-1, keepdims=True)
        dlt = acc - mu
        var = jnp.mean(dlt * dlt, axis=-1, keepdims=True)
        y = dlt * lax.rsqrt(var + EPS) * g_ref[...] + beta_ref[...]
        o_ref[r * rc:(r + 1) * rc, :] = _silu(y).astype(BF16)


def _conv_module(a3, conv_w, conv_b, cn_g, cn_b, ts=512, rc=64):
    b, s, c = a3.shape
    wpad = jnp.pad(conv_w, ((0, CONV_HALO - CONV_WIDTH), (0, 0)))
    hb = ts // CONV_HALO
    return pl.pallas_call(
        functools.partial(_conv_kernel, ts=ts, rc=rc),
        grid=(b, s // ts),
        in_specs=[
            pl.BlockSpec((None, CONV_HALO, c), lambda bi, j: (bi, jnp.maximum(j * hb - 1, 0), 0)),
            pl.BlockSpec((None, ts, c), lambda bi, j: (bi, j, 0)),
            _resident(wpad.shape), _resident((1, c)), _resident((1, c)), _resident((1, c)),
        ],
        out_specs=pl.BlockSpec((None, ts, c), lambda bi, j: (bi, j, 0)),
        out_shape=jax.ShapeDtypeStruct((b, s, c), BF16),
        scratch_shapes=[pltpu.VMEM((ts + CONV_HALO, c), F32)],
        compiler_params=_params(("parallel", "parallel")),
        name="conv_module",
    )(a3, a3, wpad, conv_b[None], cn_g[None], cn_b[None])


def _attn_kernel(q_ref, qi_ref, smq_ref, k_ref, v_ref, smk_ref, o_ref, key_scr, wb_scr,
                 *, tq, tk, seq, topk):
    j = pl.program_id(1)
    q0 = j * tq
    nkt = (q0 + tq + tk - 1) // tk
    imin = jnp.int32(INT_MIN)
    row = lax.broadcasted_iota(jnp.int32, (tq, 1), 0) + q0
    limit = (row // CHUNK + 1) * CHUNK
    lane_pos = lax.broadcasted_iota(jnp.int32, (tq, tk), 1)

    wi = smq_ref[:, IDX_DIM:IDX_DIM + IDX_HEADS]
    for h in range(IDX_HEADS):
        wb_scr[h] = jnp.broadcast_to(wi[:, h:h + 1], (tq, tk))

    def score_tile(kt, _):
        k0 = pl.multiple_of(kt * tk, tk)
        ki = smk_ref[pl.ds(k0, tk), 0:IDX_DIM].astype(BF16)
        acc = jnp.zeros((tq, tk), F32)
        for h in range(IDX_HEADS):
            d = _dot_nt(qi_ref[:, h * IDX_DIM:(h + 1) * IDX_DIM], ki)
            acc = acc + wb_scr[h] * jnp.maximum(d, 0.0)
        bits = pltpu.bitcast(acc, jnp.int32)
        skey = bits ^ ((bits >> 31) & jnp.int32(0x7FFFFFFF))
        key_scr[kt] = jnp.where(lane_pos + k0 < limit, skey, imin)
        return 0

    lax.fori_loop(0, nkt, score_tile, 0)

    def count(pred):
        def body(kt, part):
            m = jnp.where(pred(key_scr[kt], lane_pos + kt * tk), 1.0, 0.0)
            for c in range(tk // LANES):
                part = part + m[:, c * LANES:(c + 1) * LANES]
            return part
        part = lax.fori_loop(0, nkt, body, jnp.zeros((tq, LANES), F32))
        return jnp.sum(part, axis=1, keepdims=True)

    kf = jnp.float32(topk)

    def search():
        c0 = count(lambda sk, pos: sk >= 0)
        t0 = jnp.where(c0 >= kf, jnp.int32(0), imin)

        def bit_step(i, t):
            cand = t | (jnp.int32(1) << (30 - i))
            c = count(lambda sk, pos: sk >= cand)
            return jnp.where(c >= kf, cand, t)

        t = lax.fori_loop(0, 31, bit_step, t0)
        c_ge = count(lambda sk, pos: sk >= t)
        c_gt = count(lambda sk, pos: sk > t)
        need = kf - c_gt
        has_tie = jnp.max(c_ge) > kf

        def tie_search():
            def idx_step(i, v):
                cand = v | (jnp.int32(1) << (seq.bit_length() - 2 - i))
                f = count(lambda sk, pos: (sk == t) & (pos < cand))
                return jnp.where(f < need, cand, v)
            return lax.fori_loop(0, seq.bit_length() - 1, idx_step, jnp.zeros((tq, 1), jnp.int32))

        v = lax.cond(has_tie, tie_search, lambda: jnp.full((tq, 1), seq - 1, jnp.int32))
        return t, v

    t, v = lax.cond(q0 + tq > topk, search,
                    lambda: (jnp.full((tq, 1), INT_MIN, jnp.int32), jnp.full((tq, 1), seq - 1, jnp.int32)))

    def bias_tile(kt, _):
        sk = key_scr[kt]
        pos = lane_pos + kt * tk
        sel = ((sk > t) | ((sk == t) & (pos <= v))) & (sk > imin)
        key_scr[kt] = pltpu.bitcast(jnp.where(sel, 0.0, NEG_BIG).astype(F32), jnp.int32)
        return 0

    lax.fori_loop(0, nkt, bias_tile, 0)

    for h in range(ATT_HEADS):
        hs = slice(h * HEAD_DIM, (h + 1) * HEAD_DIM)
        qh = q_ref[:, hs]

        def att_tile(kt, carry, hs=hs, qh=qh):
            m, l, acc = carry
            k0 = pl.multiple_of(kt * tk, tk)
            s = _dot_nt(qh, k_ref[pl.ds(k0, tk), hs]) + pltpu.bitcast(key_scr[kt], F32)
            m_new = jnp.maximum(m, jnp.max(s, axis=1, keepdims=True))
            alpha = jnp.exp(m - m_new)
            p = jnp.exp(s - m_new)
            l = alpha * l + jnp.sum(p, axis=1, keepdims=True)
            acc = alpha * acc + _dot(p.astype(BF16), v_ref[pl.ds(k0, tk), hs])
            return m_new, l, acc

        m, l, acc = lax.fori_loop(
            0, nkt, att_tile,
            (jnp.full((tq, 1), NEG_BIG, F32), jnp.zeros((tq, 1), F32), jnp.zeros((tq, HEAD_DIM), F32)))
        o_ref[:, hs] = (acc / l).astype(BF16)


def _attention(q, k, v, qi, sm, batch, seq, tq=256, tk=512):
    m = q.shape[0]
    topk = min(TOPK_MAX, seq // 4)
    nq = seq // tq
    qrow = lambda n: pl.BlockSpec((tq, n), lambda b, j: (b * nq + j, 0))
    krow = lambda n: pl.BlockSpec((seq, n), lambda b, j: (b, 0))
    return pl.pallas_call(
        functools.partial(_attn_kernel, tq=tq, tk=tk, seq=seq, topk=topk),
        grid=(batch, nq),
        in_specs=[qrow(ATT_WIDTH), qrow(ATT_WIDTH), qrow(LANES), krow(ATT_WIDTH), krow(ATT_WIDTH), krow(LANES)],
        out_specs=qrow(ATT_WIDTH),
        out_shape=jax.ShapeDtypeStruct((m, ATT_WIDTH), BF16),
        scratch_shapes=[pltpu.VMEM((seq // tk, tq, tk), jnp.int32),
                        pltpu.VMEM((IDX_HEADS, tq, tk), F32)],
        compiler_params=_params(("parallel", "arbitrary")),
        name="dsa_attention",
    )(q, qi, sm, k, v, sm)


def _mid_kernel(a_ref, att_ref, x_ref, mod_ref, g_ref, wa_ref, wb_ref, wg_ref, wu_ref, wd_ref, o_ref, *, fc):
    mod = mod_ref[...]
    mix = _dot(a_ref[...], wa_ref[...]) + _dot(att_ref[...], wb_ref[...])
    x1 = x_ref[...] + mod[2:3] * mix
    h = _norm_mod(x1, g_ref[...], mod[4:5], mod[3:4]).astype(BF16)
    y = jnp.zeros(x1.shape, F32)
    for c in range(D_FF // fc):
        cs = slice(c * fc, (c + 1) * fc)
        act = (_silu(_dot(h, wg_ref[:, cs])) * _dot(h, wu_ref[:, cs])).astype(BF16)
        y = y + _dot(act, wd_ref[cs, :])
    o_ref[...] = x1 + mod[5:6] * y


def _mid0(a, att, x2, mod, g, w_out, w_gate, w_up, w_down, seq, tm):
    m, d = x2.shape
    wa = w_out[:CONV_CH].astype(BF16)
    wb = w_out[CONV_CH:].astype(BF16)
    wg, wu, wd = w_gate.astype(BF16), w_up.astype(BF16), w_down.astype(BF16)
    row = lambda n: pl.BlockSpec((tm, n), lambda i: (i, 0))
    return pl.pallas_call(
        functools.partial(_mid_kernel, fc=D_FF // 2),
        grid=(m // tm,),
        in_specs=[row(CONV_CH), row(ATT_WIDTH), row(d), _mod_spec(0, seq // tm), _resident((1, d)),
                  _resident(wa.shape), _resident(wb.shape), _resident(wg.shape), _resident(wu.shape),
                  _resident(wd.shape)],
        out_specs=row(d),
        out_shape=jax.ShapeDtypeStruct((m, d), F32),
        compiler_params=_params(("parallel",)),
        name="outproj_ffn0",
    )(a, att, x2, mod, g[None], wa, wb, wg, wu, wd)


def _sc_kernel(x_ref, mod_ref, g1_ref, g2_ref, win_ref, cw_ref, wout_ref, wr_ref, tri_ref,
               x3_ref, h3_ref, ri_ref, rw_ref, cnt_ref, buf, cnt_scr, *, tm, tiles_per_seq):
    i = pl.program_id(0)
    mod = mod_ref[...]
    w = SC_WIDTH

    @pl.when(i == 0)
    def _():
        cnt_scr[...] = jnp.zeros_like(cnt_scr)

    @pl.when(i % tiles_per_seq == 0)
    def _():
        buf[0:SC_HALO, :] = jnp.zeros((SC_HALO, w), F32)

    x = x_ref[...]
    h = _norm_mod(x, g1_ref[...], mod[1:2], mod[0:1]).astype(BF16)
    bg = _dot(h, win_ref[:, 0:w])
    buf[SC_HALO:, :] = _dot(h, win_ref[:, w:2 * w]) * _dot(h, win_ref[:, 2 * w:3 * w])
    cw = cw_ref[...]
    conv = (cw[0:1] * buf[SC_HALO - 2:SC_HALO - 2 + tm, :]
            + cw[1:2] * buf[SC_HALO - 1:SC_HALO - 1 + tm, :]
            + cw[2:3] * buf[SC_HALO:SC_HALO + tm, :])
    buf[0:SC_HALO, :] = buf[tm:tm + SC_HALO, :]
    x3 = x + mod[2:3] * _dot((bg * conv).astype(BF16), wout_ref[...])
    x3_ref[...] = x3
    h3 = _norm_mod(x3, g2_ref[...], mod[4:5], mod[3:4])
    h3_ref[...] = h3

    logits = _dot3(h3, wr_ref[...])
    lane = lax.broadcasted_iota(jnp.int32, logits.shape, 1)
    lane_f = lane.astype(F32)
    logits = jnp.where(lane < N_EXPERTS, logits, -jnp.inf)
    m1 = jnp.max(logits, axis=1, keepdims=True)
    i1 = jnp.min(jnp.where(logits == m1, lane_f, float(LANES)), axis=1, keepdims=True)
    rest = jnp.where(lane_f == i1, -jnp.inf, logits)
    m2 = jnp.max(rest, axis=1, keepdims=True)
    i2 = jnp.min(jnp.where(rest == m2, lane_f, float(LANES)), axis=1, keepdims=True)
    e = jnp.exp(m2 - m1)
    w1 = 1.0 / (1.0 + e)
    w2 = e / (1.0 + e)

    oh1 = lane_f == i1
    oh2 = lane_f == i2
    oh = jnp.where(oh1 | oh2, 1.0, 0.0)
    before = _dot(tri_ref[...], oh.astype(BF16)) + cnt_scr[...]
    r1 = jnp.sum(jnp.where(oh1, before, 0.0), axis=1, keepdims=True)
    r2 = jnp.sum(jnp.where(oh2, before, 0.0), axis=1, keepdims=True)
    cnt_scr[...] = cnt_scr[...] + jnp.sum(oh, axis=0, keepdims=True)
    cnt_ref[...] = jnp.broadcast_to(cnt_scr[...], cnt_ref.shape)

    r1i = r1.astype(jnp.int32)
    r2i = r2.astype(jnp.int32)
    ri_ref[...] = jnp.where(lane == 0, i1.astype(jnp.int32),
                            jnp.where(lane == 1, i2.astype(jnp.int32), jnp.where(lane == 2, r1i, r2i)))
    rw_ref[...] = jnp.where(lane == 0, w1, w2)


def _sc_layer(x2, mod, g1, g2, w_in, conv_w, w_out, w_router, seq, tm):
    m, d = x2.shape
    win = w_in.astype(BF16)
    wout = w_out.astype(BF16)
    wr = jnp.pad(w_router, ((0, 0), (0, LANES - N_EXPERTS)))
    cw = jnp.pad(conv_w, ((0, 8 - conv_w.shape[0]), (0, 0)))
    idx = jnp.arange(tm)
    tri = (idx[None, :] < idx[:, None]).astype(BF16)
    row = lambda n: pl.BlockSpec((tm, n), lambda i: (i, 0))
    return pl.pallas_call(
        functools.partial(_sc_kernel, tm=tm, tiles_per_seq=seq // tm),
        grid=(m // tm,),
        in_specs=[row(d), _mod_spec(1, seq // tm), _resident((1, d)), _resident((1, d)), _resident(win.shape),
                  _resident(cw.shape), _resident(wout.shape), _resident(wr.shape), _resident(tri.shape)],
        out_specs=[row(d), row(d), row(LANES), row(LANES), pl.BlockSpec((8, LANES), lambda i: (0, 0))],
        out_shape=[jax.ShapeDtypeStruct((m, d), F32), jax.ShapeDtypeStruct((m, d), F32),
                   jax.ShapeDtypeStruct((m, LANES), jnp.int32), jax.ShapeDtypeStruct((m, LANES), F32),
                   jax.ShapeDtypeStruct((8, LANES), F32)],
        scratch_shapes=[pltpu.VMEM((tm + SC_HALO, SC_WIDTH), F32), pltpu.VMEM((1, LANES), F32)],
        compiler_params=_params(("arbitrary",)),
        name="shortconv_router",
    )(x2, mod, g1[None], g2[None], win, cw, wout, wr, tri)


def _dispatch_kernel(p1_ref, p2_ref, h_ref, xs_in_ref, xs_ref, sems, *, td):
    del xs_in_ref

    def issue(t, _):
        src = h_ref.at[pl.ds(t, 1)]
        pltpu.make_async_copy(src, xs_ref.at[pl.ds(p1_ref[t], 1)], sems.at[0]).start()
        pltpu.make_async_copy(src, xs_ref.at[pl.ds(p2_ref[t], 1)], sems.at[1]).start()
        return 0

    lax.fori_loop(0, td, issue, 0)
    pltpu.make_async_copy(h_ref, xs_ref.at[pl.ds(0, td)], sems.at[0]).wait()
    pltpu.make_async_copy(h_ref, xs_ref.at[pl.ds(0, td)], sems.at[1]).wait()


def _dispatch(h3, pos1, pos2, rows, td=1024):
    m, d = h3.shape
    smem = lambda: pl.BlockSpec((td,), lambda i: (i,), memory_space=pltpu.SMEM)
    return pl.pallas_call(
        functools.partial(_dispatch_kernel, td=td),
        grid=(m // td,),
        in_specs=[smem(), smem(), pl.BlockSpec((td, d), lambda i: (i, 0)), pl.BlockSpec(memory_space=pl.ANY)],
        out_specs=pl.BlockSpec(memory_space=pl.ANY),
        out_shape=jax.ShapeDtypeStruct((rows, d), F32),
        scratch_shapes=[pltpu.SemaphoreType.DMA((2,))],
        input_output_aliases={3: 0},
        compiler_params=_params(("arbitrary",)),
        name="moe_dispatch",
    )(pos1, pos2, h3, jnp.zeros((rows, d), F32))


def _moe_kernel(te_ref, nu_ref, xs_ref, wg_ref, wu_ref, wd_ref, ys_ref, *, fc):
    del te_ref
    i = pl.program_id(0)

    @pl.when(i < nu_ref[0])
    def _():
        x = xs_ref[...].astype(BF16)
        y = jnp.zeros(ys_ref.shape, F32)
        for c in range(D_FF // fc):
            cs = slice(c * fc, (c + 1) * fc)
            act = (_silu(_dot(x, wg_ref[:, cs])) * _dot(x, wu_ref[:, cs])).astype(BF16)
            y = y + _dot(act, wd_ref[cs, :])
        ys_ref[...] = y

    @pl.when(i >= nu_ref[0])
    def _():
        ys_ref[...] = jnp.zeros(ys_ref.shape, F32)


def _moe_ffn(xs, tile_expert, n_used, w_gate, w_up, w_down, tm):
    rows, d = xs.shape
    wg, wu, wd = w_gate.astype(BF16), w_up.astype(BF16), w_down.astype(BF16)
    wspec = lambda shape: pl.BlockSpec((None,) + shape, lambda i, te, nu: (te[i], 0, 0),
                                       pipeline_mode=pl.Buffered(1))
    return pl.pallas_call(
        functools.partial(_moe_kernel, fc=D_FF // 2),
        grid_spec=pltpu.PrefetchScalarGridSpec(
            num_scalar_prefetch=2,
            grid=(rows // tm,),
            in_specs=[pl.BlockSpec((tm, d), lambda i, te, nu: (i, 0)),
                      wspec((d, D_FF)), wspec((d, D_FF)), wspec((D_FF, d))],
            out_specs=pl.BlockSpec((tm, d), lambda i, te, nu: (i, 0)),
        ),
        out_shape=jax.ShapeDtypeStruct((rows, d), F32),
        compiler_params=_params(("arbitrary",)),
        name="moe_experts",
    )(tile_expert, n_used, xs, wg, wu, wd)


def _combine_kernel(p1_ref, p2_ref, ys_ref, x_ref, rw_ref, mod_ref, o_ref, buf, sems, *, tc):
    def issue(t, _):
        pltpu.make_async_copy(ys_ref.at[pl.ds(p1_ref[t], 1)], buf.at[0, pl.ds(t, 1)], sems.at[0]).start()
        pltpu.make_async_copy(ys_ref.at[pl.ds(p2_ref[t], 1)], buf.at[1, pl.ds(t, 1)], sems.at[1]).start()
        return 0

    lax.fori_loop(0, tc, issue, 0)
    pltpu.make_async_copy(ys_ref.at[pl.ds(0, tc)], buf.at[0], sems.at[0]).wait()
    pltpu.make_async_copy(ys_ref.at[pl.ds(0, tc)], buf.at[1], sems.at[1]).wait()
    rw = rw_ref[...]
    moe = rw[:, 0:1] * buf[0] + rw[:, 1:2] * buf[1]
    o_ref[...] = x_ref[...] + mod_ref[...][5:6] * moe


def _combine(ys, pos1, pos2, x3, rw, mod, seq, tc=1024):
    m, d = x3.shape
    smem = lambda: pl.BlockSpec((tc,), lambda i: (i,), memory_space=pltpu.SMEM)
    row = lambda n: pl.BlockSpec((tc, n), lambda i: (i, 0))
    return pl.pallas_call(
        functools.partial(_combine_kernel, tc=tc),
        grid=(m // tc,),
        in_specs=[smem(), smem(), pl.BlockSpec(memory_space=pl.ANY), row(d), row(LANES), _mod_spec(1, seq // tc)],
        out_specs=row(d),
        out_shape=jax.ShapeDtypeStruct((m, d), F32),
        scratch_shapes=[pltpu.VMEM((2, tc, d), F32), pltpu.SemaphoreType.DMA((2,))],
        compiler_params=_params(("arbitrary",)),
        name="moe_combine",
    )(pos1, pos2, ys, x3, rw, mod)


def _moe_layer(x3, h3, ri, rw, counts, mod, w_gate, w_up, w_down, seq, tm=512):
    m, d = x3.shape
    n_tiles = (2 * m) // tm + N_EXPERTS
    cnt = counts[0, :N_EXPERTS].astype(jnp.int32)
    padded = ((cnt + tm - 1) // tm) * tm
    ends = jnp.cumsum(padded)
    offs = ends - padded
    pos1 = offs[ri[:, 0]] + ri[:, 2]
    pos2 = offs[ri[:, 1]] + ri[:, 3]
    tile_start = jnp.arange(n_tiles, dtype=jnp.int32) * tm
    tile_expert = jnp.minimum(jnp.searchsorted(ends, tile_start, side="right"), N_EXPERTS - 1).astype(jnp.int32)
    n_used = (ends[-1:] // tm).astype(jnp.int32)
    xs = _dispatch(h3, pos1, pos2, n_tiles * tm)
    ys = _moe_ffn(xs, tile_expert, n_used, w_gate, w_up, w_down, tm)
    return _combine(ys, pos1, pos2, x3, rw, mod, seq)


def kernel(x, c, ada_w, ada_b, norm_g, ab_w_in, ab_conv_w, ab_conv_b, ab_cnorm_g, ab_cnorm_b, ab_q_g, ab_k_g,
           ab_w_out, ffn_w_gate, ffn_w_up, ffn_w_down, sc_w_in, sc_conv_w, sc_w_out, moe_router, moe_w_gate,
           moe_w_up, moe_w_down):
    batch, seq, d = x.shape
    assert d == D_MODEL and ada_w.shape[0] == 2 and seq % 1024 == 0
    tm = 512
    x2 = x.reshape(batch * seq, d)
    mod = _adaln(c, ada_w, ada_b)

    a, q, k, v, qi, sm = _in_proj0(x2, mod, norm_g[0, 0], ab_w_in[0], ab_q_g[0], ab_k_g[0], seq, tm)
    a = _conv_module(a.reshape(batch, seq, CONV_CH), ab_conv_w[0], ab_conv_b[0], ab_cnorm_g[0], ab_cnorm_b[0])
    att = _attention(q, k, v, qi, sm, batch, seq)
    x2 = _mid0(a.reshape(batch * seq, CONV_CH), att, x2, mod, norm_g[0, 1], ab_w_out[0],
               ffn_w_gate[0], ffn_w_up[0], ffn_w_down[0], seq, tm)

    x3, h3, ri, rw, counts = _sc_layer(x2, mod, norm_g[1, 0], norm_g[1, 1], sc_w_in[0], sc_conv_w[0],
                                       sc_w_out[0], moe_router[0], seq, tm)
    out = _moe_layer(x3, h3, ri, rw, counts, mod, moe_w_gate[0], moe_w_up[0], moe_w_down[0], seq)
    return out.reshape(batch, seq, d)
```

```python
import functools

import jax
import jax.numpy as jnp
from jax import lax
from jax.experimental import pallas as pl
from jax.experimental.pallas import tpu as pltpu

D_MODEL = 1024
CHUNK = 64
CONV_CH = 512
CONV_WIDTH = 31
ATT_HEADS = 8
HEAD_DIM = 64
ATT_WIDTH = ATT_HEADS * HEAD_DIM
IDX_HEADS = 8
IDX_DIM = 64
TOPK_MAX = 256
SC_WIDTH = 1024
D_FF = 2816
N_EXPERTS = 8
EPS = 1e-6

LANES = 128
CONV_HALO = 32
SC_HALO = 8
NEG_BIG = -1e30
INT_MIN = -2 ** 31
LOG2_E = 1.4426950408889634
COUNT_ROWS = 64
MAX_CONST_SHIFT = 60.0
VMEM_LIMIT = 56 * 1024 * 1024

BF16 = jnp.bfloat16
F32 = jnp.float32


def _dot(a, b):
    return jnp.dot(a, b, preferred_element_type=F32)


def _dot_nt(a, b):
    return lax.dot_general(a, b, (((1,), (1,)), ((), ())), preferred_element_type=F32)


def _split_bf16(a):
    hi = a.astype(BF16)
    lo = (a - hi.astype(F32)).astype(BF16)
    return hi, lo


def _dot3(a, b):
    a_hi, a_lo = _split_bf16(a)
    b_hi, b_lo = _split_bf16(b)
    return _dot(a_hi, b_hi) + (_dot(a_hi, b_lo) + _dot(a_lo, b_hi))


def _silu(x):
    return x * jax.nn.sigmoid(x)


def _norm_mod(x, g, scale, shift):
    ms = jnp.mean(x * x, axis=-1, keepdims=True)
    return (x * lax.rsqrt(ms + EPS) * g) * (1.0 + scale) + shift


def _params(sem, vmem=VMEM_LIMIT):
    return pltpu.CompilerParams(dimension_semantics=sem, vmem_limit_bytes=vmem)


def _resident(shape):
    nd = len(shape)
    return pl.BlockSpec(shape, lambda *_: (0,) * nd, pipeline_mode=pl.Buffered(1))


def _ada_kernel(c_ref, w_ref, b_ref, o_ref):
    o_ref[...] = _dot3(_silu(c_ref[...]), w_ref[...]) + b_ref[...]


def _adaln(c, ada_w, ada_b):
    depth, d, n = ada_w.shape
    b = c.shape[0]
    tn = 1536
    mod = pl.pallas_call(
        _ada_kernel,
        grid=(depth, n // tn),
        in_specs=[
            pl.BlockSpec((b, d), lambda l, j: (0, 0)),
            pl.BlockSpec((None, d, tn), lambda l, j: (l, 0, j)),
            pl.BlockSpec((None, 1, tn), lambda l, j: (l, 0, j)),
        ],
        out_specs=pl.BlockSpec((None, b, tn), lambda l, j: (l, 0, j)),
        out_shape=jax.ShapeDtypeStruct((depth, b, n), F32),
        compiler_params=_params(("arbitrary", "arbitrary")),
        name="adaln",
    )(c, ada_w, ada_b.reshape(depth, 1, n))
    return mod.reshape(depth, b, 6, d)


def _mod_spec(layer, rows_per_batch_tiles):
    return pl.BlockSpec((None, None, 6, D_MODEL),
                        lambda i, *_: (layer, i // rows_per_batch_tiles, 0, 0))


def _in0_kernel(x_ref, mod_ref, g_ref, wm_ref, ws_ref, gm_ref, qg_ref, kg_ref,
                a_ref, q_ref, k_ref, v_ref, qi_ref, sm_ref):
    mod = mod_ref[...]
    h = _norm_mod(x_ref[...], g_ref[...], mod[1:2], mod[0:1]).astype(BF16)
    c = CONV_CH
    u = _dot(h, wm_ref[:, 0:2 * c])
    a_ref[...] = u[:, :c] * jax.nn.sigmoid(u[:, c:])

    def head_norm(t, g):
        ss = _dot((t * t).astype(BF16), gm_ref[...])
        return t * lax.rsqrt(ss * (1.0 / HEAD_DIM) + EPS) * g

    o = 2 * c
    w = ATT_WIDTH
    q = head_norm(_dot(h, wm_ref[:, o:o + w]), qg_ref[...]).astype(BF16)
    k_ref[...] = head_norm(_dot(h, wm_ref[:, o + w:o + 2 * w]), kg_ref[...]).astype(BF16)
    v = _dot(h, wm_ref[:, o + 2 * w:o + 3 * w]).astype(BF16)
    qi_ref[...] = _dot(h, wm_ref[:, o + 3 * w:o + 4 * w]).astype(BF16)
    low = lax.broadcasted_iota(jnp.int32, (q.shape[0], LANES), 1) < HEAD_DIM
    zero = jnp.zeros((), BF16)
    one = jnp.ones((), BF16)
    for p in range(ATT_HEADS // 2):
        ps = slice(p * LANES, (p + 1) * LANES)
        e = slice(2 * p * LANES, (2 * p + 1) * LANES)
        od = slice((2 * p + 1) * LANES, (2 * p + 2) * LANES)
        q_ref[:, e] = jnp.where(low, q[:, ps], zero)
        q_ref[:, od] = jnp.where(low, zero, q[:, ps])
        v_ref[:, e] = jnp.where(low, v[:, ps], one)
        v_ref[:, od] = jnp.where(low, one, v[:, ps])
    sm = _dot(h, ws_ref[...])
    lane = lax.broadcasted_iota(jnp.int32, sm.shape, 1)
    is_wi = (lane >= IDX_DIM) & (lane < IDX_DIM + IDX_HEADS)
    sm_ref[...] = jnp.where(is_wi, sm * ((IDX_DIM * IDX_HEADS) ** -0.5), sm)


def _in_proj0(x2, mod, g, w_in, q_g, k_g, seq, tm):
    m, d = x2.shape
    nmain = 2 * CONV_CH + 3 * ATT_WIDTH + IDX_HEADS * IDX_DIM
    nsmall = IDX_DIM + IDX_HEADS
    wm = w_in[:, :nmain].astype(BF16)
    ws = jnp.pad(w_in[:, nmain:nmain + nsmall], ((0, 0), (0, LANES - nsmall))).astype(BF16)
    hid = jnp.arange(ATT_WIDTH) // HEAD_DIM
    gm = (hid[:, None] == hid[None, :]).astype(BF16)
    qg = (jnp.tile(q_g, ATT_HEADS) * (HEAD_DIM ** -0.5 * LOG2_E))[None]
    kg = jnp.tile(k_g, ATT_HEADS)[None]
    row = lambda n: pl.BlockSpec((tm, n), lambda i: (i, 0))
    bound = (jnp.max(jnp.abs(qg)) * jnp.max(jnp.abs(kg)) * (HEAD_DIM * 1.01)).reshape(1)
    outs = pl.pallas_call(
        _in0_kernel,
        grid=(m // tm,),
        in_specs=[row(d), _mod_spec(0, seq // tm), _resident((1, d)), _resident(wm.shape),
                  _resident(ws.shape), _resident(gm.shape), _resident(qg.shape), _resident(kg.shape)],
        out_specs=[row(CONV_CH), row(2 * ATT_WIDTH), row(ATT_WIDTH), row(2 * ATT_WIDTH), row(ATT_WIDTH), row(LANES)],
        out_shape=[jax.ShapeDtypeStruct((m, CONV_CH), F32),
                   jax.ShapeDtypeStruct((m, 2 * ATT_WIDTH), BF16), jax.ShapeDtypeStruct((m, ATT_WIDTH), BF16),
                   jax.ShapeDtypeStruct((m, 2 * ATT_WIDTH), BF16), jax.ShapeDtypeStruct((m, ATT_WIDTH), BF16),
                   jax.ShapeDtypeStruct((m, LANES), F32)],
        compiler_params=_params(("parallel",)),
        name="in_proj0",
    )(x2, mod, g[None], wm, ws, gm, qg, kg)
    return outs, bound


def _conv_kernel(prev_ref, cur_ref, w_ref, b_ref, g_ref, beta_ref, o_ref, buf, *, ts, rc):
    j = pl.program_id(1)
    buf[0:CONV_HALO, :] = jnp.where(j > 0, prev_ref[...], 0.0)
    buf[CONV_HALO:, :] = cur_ref[...]
    w = w_ref[...]
    lead = CONV_HALO - (CONV_WIDTH - 1)
    for r in range(ts // rc):
        acc = jnp.broadcast_to(b_ref[...], (rc, CONV_CH))
        for k in range(CONV_WIDTH):
            s = r * rc + k + lead
            acc = acc + w[k:k + 1, :] * buf[s:s + rc, :]
        mu = jnp.mean(acc, axis=-1, keepdims=True)
        dlt = acc - mu
        var = jnp.mean(dlt * dlt, axis=-1, keepdims=True)
        y = dlt * lax.rsqrt(var + EPS) * g_ref[...] + beta_ref[...]
        o_ref[r * rc:(r + 1) * rc, :] = _silu(y).astype(BF16)


def _conv_module(a3, conv_w, conv_b, cn_g, cn_b, ts=512, rc=64):
    b, s, c = a3.shape
    wpad = jnp.pad(conv_w, ((0, CONV_HALO - CONV_WIDTH), (0, 0)))
    hb = ts // CONV_HALO
    return pl.pallas_call(
        functools.partial(_conv_kernel, ts=ts, rc=rc),
        grid=(b, s // ts),
        in_specs=[
            pl.BlockSpec((None, CONV_HALO, c), lambda bi, j: (bi, jnp.maximum(j * hb - 1, 0), 0)),
            pl.BlockSpec((None, ts, c), lambda bi, j: (bi, j, 0)),
            _resident(wpad.shape), _resident((1, c)), _resident((1, c)), _resident((1, c)),
        ],
        out_specs=pl.BlockSpec((None, ts, c), lambda bi, j: (bi, j, 0)),
        out_shape=jax.ShapeDtypeStruct((b, s, c), BF16),
        scratch_shapes=[pltpu.VMEM((ts + CONV_HALO, c), F32)],
        compiler_params=_params(("parallel", "parallel")),
        name="conv_module",
    )(a3, a3, wpad, conv_b[None], cn_g[None], cn_b[None])


def _attn_kernel(bnd_ref, q_ref, qi_ref, smq_ref, k_ref, v_ref, smk_ref, o_ref,
                 key_scr, wb_scr, st_scr, m_scr, acc_scr, *, tq, tk, seq, topk):
    j = pl.program_id(1)
    q0 = j * tq
    nkt = (q0 + tq + tk - 1) // tk
    imin = jnp.int32(INT_MIN)
    row = lax.broadcasted_iota(jnp.int32, (tq, 1), 0) + q0
    limit = (row // CHUNK + 1) * CHUNK
    lane_pos = lax.broadcasted_iota(jnp.int32, (tq, tk), 1)

    wi = smq_ref[:, IDX_DIM:IDX_DIM + IDX_HEADS]
    for h in range(IDX_HEADS):
        wb_scr[h] = jnp.broadcast_to(wi[:, h:h + 1], (tq, tk))

    def score_tile(kt, _):
        k0 = pl.multiple_of(kt * tk, tk)
        ki = smk_ref[pl.ds(k0, tk), 0:IDX_DIM].astype(BF16)
        acc = jnp.zeros((tq, tk), F32)
        for h in range(IDX_HEADS):
            d = _dot_nt(qi_ref[:, h * IDX_DIM:(h + 1) * IDX_DIM], ki)
            acc = acc + wb_scr[h] * jnp.maximum(d, 0.0)
        bits = pltpu.bitcast(acc, jnp.int32)
        skey = bits ^ ((bits >> 31) & jnp.int32(0x7FFFFFFF))
        key_scr[kt] = jnp.where(lane_pos + k0 < limit, skey, imin)
        return 0

    lax.fori_loop(0, nkt, score_tile, 0)

    lane128 = lax.broadcasted_iota(jnp.int32, (tq, LANES), 1)

    kf = jnp.float32(topk)
    rows = COUNT_ROWS
    row_chunks = [slice(r * rows, (r + 1) * rows) for r in range(tq // rows)]
    lane_r = lax.broadcasted_iota(jnp.int32, (rows, LANES), 1)
    ones = jnp.ones((LANES, LANES), BF16)
    t_scr, cand_scr, v_scr = st_scr.at[0], st_scr.at[1], st_scr.at[2]

    def count(pred, *state):
        def body(kt, parts):
            out = []
            for rs, part in zip(row_chunks, parts):
                ops = [r[rs, :] for r in state]
                for c in range(tk // LANES):
                    sk = key_scr[kt, rs, c * LANES:(c + 1) * LANES]
                    part = part + jnp.where(pred(sk, lane_r + (kt * tk + c * LANES), *ops), 1.0, 0.0)
                out.append(part)
            return tuple(out)
        parts = lax.fori_loop(0, nkt, body, tuple(jnp.zeros((rows, LANES), F32) for _ in row_chunks))
        return _dot(jnp.concatenate(parts, axis=0).astype(BF16), ones)

    @pl.when(q0 + tq > topk)
    def _():
        c0 = count(lambda sk, pos: sk >= 0)
        t_scr[...] = jnp.where(c0 >= kf, jnp.int32(0), imin)

        def bit_step(i, _):
            cand_scr[...] = t_scr[...] | (jnp.int32(1) << (30 - i))
            c = count(lambda sk, pos, cd: sk >= cd, cand_scr)
            t_scr[...] = jnp.where(c >= kf, cand_scr[...], t_scr[...])
            return 0

        lax.fori_loop(0, 31, bit_step, 0)
        c_ge = count(lambda sk, pos, tt: sk >= tt, t_scr)
        c_gt = count(lambda sk, pos, tt: sk > tt, t_scr)
        need = kf - c_gt
        has_tie = jnp.max(c_ge) > kf

        @pl.when(has_tie)
        def _():
            v_scr[...] = jnp.zeros((tq, LANES), jnp.int32)

            def idx_step(i, _):
                cand_scr[...] = v_scr[...] | (jnp.int32(1) << (seq.bit_length() - 2 - i))
                f = count(lambda sk, pos, tt, cd: (sk == tt) & (pos < cd), t_scr, cand_scr)
                v_scr[...] = jnp.where(f < need, cand_scr[...], v_scr[...])
                return 0

            lax.fori_loop(0, seq.bit_length() - 1, idx_step, 0)

        @pl.when(jnp.logical_not(has_tie))
        def _():
            v_scr[...] = jnp.full((tq, LANES), seq - 1, jnp.int32)

    @pl.when(q0 + tq <= topk)
    def _():
        t_scr[...] = jnp.full((tq, LANES), INT_MIN, jnp.int32)
        v_scr[...] = jnp.full((tq, LANES), seq - 1, jnp.int32)

    bound = bnd_ref[0]
    const_shift = bound <= MAX_CONST_SHIFT
    shift = jnp.where(const_shift, bound, 0.0)

    def bias_tile(kt, _):
        for rs in row_chunks:
            t = t_scr[rs, :]
            v = v_scr[rs, :]
            for c in range(tk // LANES):
                cs = slice(c * LANES, (c + 1) * LANES)
                sk = key_scr[kt, rs, cs]
                pos = lane_r + (kt * tk + c * LANES)
                sel = ((sk > t) | ((sk == t) & (pos <= v))) & (sk > imin)
                key_scr[kt, rs, cs] = pltpu.bitcast(jnp.where(sel, -shift, NEG_BIG).astype(F32), jnp.int32)
        return 0

    lax.fori_loop(0, nkt, bias_tile, 0)

    acc_scr[...] = jnp.zeros(acc_scr.shape, F32)

    def logits(kt, h):
        k0 = pl.multiple_of(kt * tk, tk)
        ps = slice((h // 2) * LANES, (h // 2 + 1) * LANES)
        return (_dot_nt(q_ref[:, h * LANES:(h + 1) * LANES], k_ref[pl.ds(k0, tk), ps])
                + pltpu.bitcast(key_scr[kt], F32))

    def values(kt, h):
        return v_ref[pl.ds(pl.multiple_of(kt * tk, tk), tk), h * LANES:(h + 1) * LANES]

    def att_tile_const(kt, _):
        for h in range(ATT_HEADS):
            p = jnp.exp2(logits(kt, h)).astype(BF16)
            acc_scr[h] = acc_scr[h] + _dot(p, values(kt, h))
        return 0

    def att_tile_online(kt, _):
        for h in range(ATT_HEADS):
            s = logits(kt, h)
            m_old = m_scr[h]
            m_new = jnp.maximum(m_old, jnp.max(s, axis=1, keepdims=True))
            alpha = jnp.exp2(m_old - m_new)
            p = jnp.concatenate(
                [jnp.exp2(s[:, c * LANES:(c + 1) * LANES] - m_new) for c in range(tk // LANES)], axis=1)
            acc_scr[h] = alpha * acc_scr[h] + _dot(p.astype(BF16), values(kt, h))
            m_scr[h] = m_new
        return 0

    @pl.when(const_shift)
    def _():
        lax.fori_loop(0, nkt, att_tile_const, 0)

    @pl.when(jnp.logical_not(const_shift))
    def _():
        m_scr[...] = jnp.full(m_scr.shape, NEG_BIG, F32)
        lax.fori_loop(0, nkt, att_tile_online, 0)

    low = lane128 < HEAD_DIM
    for p in range(ATT_HEADS // 2):
        ae = acc_scr[2 * p]
        ao = acc_scr[2 * p + 1]
        oe = ae / pltpu.roll(ae, HEAD_DIM, 1)
        oo = ao / pltpu.roll(ao, HEAD_DIM, 1)
        o_ref[:, p * LANES:(p + 1) * LANES] = jnp.where(low, oe, oo).astype(BF16)


def _attention(q, k, v, qi, sm, bound, batch, seq, tq=256, tk=512):
    m = q.shape[0]
    topk = min(TOPK_MAX, seq // 4)
    nq = seq // tq
    qrow = lambda n: pl.BlockSpec((tq, n), lambda b, j: (b * nq + j, 0))
    krow = lambda n: pl.BlockSpec((seq, n), lambda b, j: (b, 0))
    return pl.pallas_call(
        functools.partial(_attn_kernel, tq=tq, tk=tk, seq=seq, topk=topk),
        grid=(batch, nq),
        in_specs=[pl.BlockSpec(memory_space=pltpu.SMEM),
                  qrow(2 * ATT_WIDTH), qrow(ATT_WIDTH), qrow(LANES),
                  krow(ATT_WIDTH), krow(2 * ATT_WIDTH), krow(LANES)],
        out_specs=qrow(ATT_WIDTH),
        out_shape=jax.ShapeDtypeStruct((m, ATT_WIDTH), BF16),
        scratch_shapes=[pltpu.VMEM((seq // tk, tq, tk), jnp.int32),
                        pltpu.VMEM((IDX_HEADS, tq, tk), F32),
                        pltpu.VMEM((3, tq, LANES), jnp.int32),
                        pltpu.VMEM((ATT_HEADS, tq, LANES), F32),
                        pltpu.VMEM((ATT_HEADS, tq, LANES), F32)],
        compiler_params=_params(("parallel", "arbitrary")),
        name="dsa_attention",
    )(bound, q, qi, sm, k, v, sm)


def _mid_kernel(a_ref, att_ref, x_ref, mod_ref, g_ref, wa_ref, wb_ref, wg_ref, wu_ref, wd_ref, o_ref, *, fc):
    mod = mod_ref[...]
    mix = _dot(a_ref[...], wa_ref[...]) + _dot(att_ref[...], wb_ref[...])
    x1 = x_ref[...] + mod[2:3] * mix
    h = _norm_mod(x1, g_ref[...], mod[4:5], mod[3:4]).astype(BF16)
    y = jnp.zeros(x1.shape, F32)
    for c in range(D_FF // fc):
        cs = slice(c * fc, (c + 1) * fc)
        act = (_silu(_dot(h, wg_ref[:, cs])) * _dot(h, wu_ref[:, cs])).astype(BF16)
        y = y + _dot(act, wd_ref[cs, :])
    o_ref[...] = x1 + mod[5:6] * y


def _mid0(a, att, x2, mod, g, w_out, w_gate, w_up, w_down, seq, tm):
    m, d = x2.shape
    wa = w_out[:CONV_CH].astype(BF16)
    wb = w_out[CONV_CH:].astype(BF16)
    wg, wu, wd = w_gate.astype(BF16), w_up.astype(BF16), w_down.astype(BF16)
    row = lambda n: pl.BlockSpec((tm, n), lambda i: (i, 0))
    return pl.pallas_call(
        functools.partial(_mid_kernel, fc=D_FF // 2),
        grid=(m // tm,),
        in_specs=[row(CONV_CH), row(ATT_WIDTH), row(d), _mod_spec(0, seq // tm), _resident((1, d)),
                  _resident(wa.shape), _resident(wb.shape), _resident(wg.shape), _resident(wu.shape),
                  _resident(wd.shape)],
        out_specs=row(d),
        out_shape=jax.ShapeDtypeStruct((m, d), F32),
        compiler_params=_params(("parallel",)),
        name="outproj_ffn0",
    )(a, att, x2, mod, g[None], wa, wb, wg, wu, wd)


def _sc_kernel(x_ref, mod_ref, g1_ref, g2_ref, win_ref, cw_ref, wout_ref, wr_ref, tri_ref,
               x3_ref, h3_ref, ri_ref, rw_ref, cnt_ref, buf, cnt_scr, *, tm, tiles_per_seq):
    i = pl.program_id(0)
    mod = mod_ref[...]
    w = SC_WIDTH

    @pl.when(i == 0)
    def _():
        cnt_scr[...] = jnp.zeros_like(cnt_scr)

    @pl.when(i % tiles_per_seq == 0)
    def _():
        buf[0:SC_HALO, :] = jnp.zeros((SC_HALO, w), F32)

    x = x_ref[...]
    h = _norm_mod(x, g1_ref[...], mod[1:2], mod[0:1]).astype(BF16)
    bg = _dot(h, win_ref[:, 0:w])
    buf[SC_HALO:, :] = _dot(h, win_ref[:, w:2 * w]) * _dot(h, win_ref[:, 2 * w:3 * w])
    cw = cw_ref[...]
    conv = (cw[0:1] * buf[SC_HALO - 2:SC_HALO - 2 + tm, :]
            + cw[1:2] * buf[SC_HALO - 1:SC_HALO - 1 + tm, :]
            + cw[2:3] * buf[SC_HALO:SC_HALO + tm, :])
    buf[0:SC_HALO, :] = buf[tm:tm + SC_HALO, :]
    x3 = x + mod[2:3] * _dot((bg * conv).astype(BF16), wout_ref[...])
    x3_ref[...] = x3
    h3 = _norm_mod(x3, g2_ref[...], mod[4:5], mod[3:4])
    h3_ref[...] = h3

    logits = _dot3(h3, wr_ref[...])
    lane = lax.broadcasted_iota(jnp.int32, logits.shape, 1)
    lane_f = lane.astype(F32)
    logits = jnp.where(lane < N_EXPERTS, logits, -jnp.inf)
    m1 = jnp.max(logits, axis=1, keepdims=True)
    i1 = jnp.min(jnp.where(logits == m1, lane_f, float(LANES)), axis=1, keepdims=True)
    rest = jnp.where(lane_f == i1, -jnp.inf, logits)
    m2 = jnp.max(rest, axis=1, keepdims=True)
    i2 = jnp.min(jnp.where(rest == m2, lane_f, float(LANES)), axis=1, keepdims=True)
    e = jnp.exp(m2 - m1)
    w1 = 1.0 / (1.0 + e)
    w2 = e / (1.0 + e)

    oh1 = lane_f == i1
    oh2 = lane_f == i2
    oh = jnp.where(oh1 | oh2, 1.0, 0.0)
    before = _dot(tri_ref[...], oh.astype(BF16)) + cnt_scr[...]
    r1 = jnp.sum(jnp.where(oh1, before, 0.0), axis=1, keepdims=True)
    r2 = jnp.sum(jnp.where(oh2, before, 0.0), axis=1, keepdims=True)
    cnt_scr[...] = cnt_scr[...] + jnp.sum(oh, axis=0, keepdims=True)
    cnt_ref[...] = jnp.broadcast_to(cnt_scr[...], cnt_ref.shape)

    r1i = r1.astype(jnp.int32)
    r2i = r2.astype(jnp.int32)
    ri_ref[...] = jnp.where(lane == 0, i1.astype(jnp.int32),
                            jnp.where(lane == 1, i2.astype(jnp.int32), jnp.where(lane == 2, r1i, r2i)))
    rw_ref[...] = jnp.where(lane == 0, w1, w2)


def _sc_layer(x2, mod, g1, g2, w_in, conv_w, w_out, w_router, seq, tm):
    m, d = x2.shape
    win = w_in.astype(BF16)
    wout = w_out.astype(BF16)
    wr = jnp.pad(w_router, ((0, 0), (0, LANES - N_EXPERTS)))
    cw = jnp.pad(conv_w, ((0, 8 - conv_w.shape[0]), (0, 0)))
    idx = jnp.arange(tm)
    tri = (idx[None, :] < idx[:, None]).astype(BF16)
    row = lambda n: pl.BlockSpec((tm, n), lambda i: (i, 0))
    return pl.pallas_call(
        functools.partial(_sc_kernel, tm=tm, tiles_per_seq=seq // tm),
        grid=(m // tm,),
        in_specs=[row(d), _mod_spec(1, seq // tm), _resident((1, d)), _resident((1, d)), _resident(win.shape),
                  _resident(cw.shape), _resident(wout.shape), _resident(wr.shape), _resident(tri.shape)],
        out_specs=[row(d), row(d), row(LANES), row(LANES), pl.BlockSpec((8, LANES), lambda i: (0, 0))],
        out_shape=[jax.ShapeDtypeStruct((m, d), F32), jax.ShapeDtypeStruct((m, d), F32),
                   jax.ShapeDtypeStruct((m, LANES), jnp.int32), jax.ShapeDtypeStruct((m, LANES), F32),
                   jax.ShapeDtypeStruct((8, LANES), F32)],
        scratch_shapes=[pltpu.VMEM((tm + SC_HALO, SC_WIDTH), F32), pltpu.VMEM((1, LANES), F32)],
        compiler_params=_params(("arbitrary",)),
        name="shortconv_router",
    )(x2, mod, g1[None], g2[None], win, cw, wout, wr, tri)


def _dispatch_kernel(ends_ref, p1_ref, p2_ref, h_ref, xs_ref, zbuf, sems, zsem, *, td, tm, n_tiles):
    @pl.when(pl.program_id(0) == 0)
    def _():
        zbuf[...] = jnp.zeros(zbuf.shape, F32)
        n_used = ends_ref[N_EXPERTS - 1] // tm
        starts, conds = [], []
        for e in range(N_EXPERTS):
            prev = ends_ref[e - 1] if e else 0
            starts.append(ends_ref[e] - tm)
            conds.append(ends_ref[e] > prev)
        for u in range(N_EXPERTS):
            starts.append((n_used + u) * tm)
            conds.append(n_used + u < n_tiles)
        zero_copy = lambda s: pltpu.make_async_copy(zbuf, xs_ref.at[pl.ds(pl.multiple_of(s, tm), tm)], zsem)
        for s, c in zip(starts, conds):
            pl.when(c)(lambda s=s: zero_copy(s).start())
        for s, c in zip(starts, conds):
            pl.when(c)(lambda s=s: zero_copy(s).wait())

    def issue(t, _):
        src = h_ref.at[pl.ds(t, 1)]
        pltpu.make_async_copy(src, xs_ref.at[pl.ds(p1_ref[t], 1)], sems.at[0]).start()
        pltpu.make_async_copy(src, xs_ref.at[pl.ds(p2_ref[t], 1)], sems.at[1]).start()
        return 0

    lax.fori_loop(0, td, issue, 0)
    pltpu.make_async_copy(h_ref, xs_ref.at[pl.ds(0, td)], sems.at[0]).wait()
    pltpu.make_async_copy(h_ref, xs_ref.at[pl.ds(0, td)], sems.at[1]).wait()


def _dispatch(h3, pos1, pos2, ends, n_tiles, tm, td=1024):
    m, d = h3.shape
    smem = lambda: pl.BlockSpec((td,), lambda i, ends: (i,), memory_space=pltpu.SMEM)
    return pl.pallas_call(
        functools.partial(_dispatch_kernel, td=td, tm=tm, n_tiles=n_tiles),
        grid_spec=pltpu.PrefetchScalarGridSpec(
            num_scalar_prefetch=1,
            grid=(m // td,),
            in_specs=[smem(), smem(), pl.BlockSpec((td, d), lambda i, ends: (i, 0))],
            out_specs=pl.BlockSpec(memory_space=pl.ANY),
            scratch_shapes=[pltpu.VMEM((tm, d), F32), pltpu.SemaphoreType.DMA((2,)), pltpu.SemaphoreType.DMA(())],
        ),
        out_shape=jax.ShapeDtypeStruct((n_tiles * tm, d), F32),
        compiler_params=_params(("arbitrary",)),
        name="moe_dispatch",
    )(ends, pos1, pos2, h3)


def _moe_kernel(te_ref, nu_ref, xs_ref, wg_ref, wu_ref, wd_ref, ys_ref, *, fc):
    del te_ref
    i = pl.program_id(0)

    @pl.when(i < nu_ref[0])
    def _():
        x = xs_ref[...].astype(BF16)
        y = jnp.zeros(ys_ref.shape, F32)
        for c in range(D_FF // fc):
            cs = slice(c * fc, (c + 1) * fc)
            act = (_silu(_dot(x, wg_ref[:, cs])) * _dot(x, wu_ref[:, cs])).astype(BF16)
            y = y + _dot(act, wd_ref[cs, :])
        ys_ref[...] = y

    @pl.when(i >= nu_ref[0])
    def _():
        ys_ref[...] = jnp.zeros(ys_ref.shape, F32)


def _moe_ffn(xs, tile_expert, n_used, w_gate, w_up, w_down, tm):
    rows, d = xs.shape
    wg, wu, wd = w_gate.astype(BF16), w_up.astype(BF16), w_down.astype(BF16)
    wspec = lambda shape: pl.BlockSpec((None,) + shape, lambda i, te, nu: (te[i], 0, 0),
                                       pipeline_mode=pl.Buffered(1))
    return pl.pallas_call(
        functools.partial(_moe_kernel, fc=D_FF // 2),
        grid_spec=pltpu.PrefetchScalarGridSpec(
            num_scalar_prefetch=2,
            grid=(rows // tm,),
            in_specs=[pl.BlockSpec((tm, d), lambda i, te, nu: (i, 0)),
                      wspec((d, D_FF)), wspec((d, D_FF)), wspec((D_FF, d))],
            out_specs=pl.BlockSpec((tm, d), lambda i, te, nu: (i, 0)),
        ),
        out_shape=jax.ShapeDtypeStruct((rows, d), F32),
        compiler_params=_params(("arbitrary",)),
        name="moe_experts",
    )(tile_expert, n_used, xs, wg, wu, wd)


def _combine_kernel(p1_ref, p2_ref, ys_ref, x_ref, rw_ref, mod_ref, o_ref, buf, sems, *, tc):
    def issue(t, _):
        pltpu.make_async_copy(ys_ref.at[pl.ds(p1_ref[t], 1)], buf.at[0, pl.ds(t, 1)], sems.at[0]).start()
        pltpu.make_async_copy(ys_ref.at[pl.ds(p2_ref[t], 1)], buf.at[1, pl.ds(t, 1)], sems.at[1]).start()
        return 0

    lax.fori_loop(0, tc, issue, 0)
    pltpu.make_async_copy(ys_ref.at[pl.ds(0, tc)], buf.at[0], sems.at[0]).wait()
    pltpu.make_async_copy(ys_ref.at[pl.ds(0, tc)], buf.at[1], sems.at[1]).wait()
    rw = rw_ref[...]
    moe = rw[:, 0:1] * buf[0] + rw[:, 1:2] * buf[1]
    o_ref[...] = x_ref[...] + mod_ref[...][5:6] * moe


def _combine(ys, pos1, pos2, x3, rw, mod, seq, tc=1024):
    m, d = x3.shape
    smem = lambda: pl.BlockSpec((tc,), lambda i: (i,), memory_space=pltpu.SMEM)
    row = lambda n: pl.BlockSpec((tc, n), lambda i: (i, 0))
    return pl.pallas_call(
        functools.partial(_combine_kernel, tc=tc),
        grid=(m // tc,),
        in_specs=[smem(), smem(), pl.BlockSpec(memory_space=pl.ANY), row(d), row(LANES), _mod_spec(1, seq // tc)],
        out_specs=row(d),
        out_shape=jax.ShapeDtypeStruct((m, d), F32),
        scratch_shapes=[pltpu.VMEM((2, tc, d), F32), pltpu.SemaphoreType.DMA((2,))],
        compiler_params=_params(("arbitrary",)),
        name="moe_combine",
    )(pos1, pos2, ys, x3, rw, mod)


def _moe_layer(x3, h3, ri, rw, counts, mod, w_gate, w_up, w_down, seq, tm=512):
    m, d = x3.shape
    n_tiles = (2 * m) // tm + N_EXPERTS
    cnt = counts[0, :N_EXPERTS].astype(jnp.int32)
    padded = ((cnt + tm - 1) // tm) * tm
    ends = jnp.cumsum(padded)
    offs = ends - padded
    pos1 = offs[ri[:, 0]] + ri[:, 2]
    pos2 = offs[ri[:, 1]] + ri[:, 3]
    tile_start = jnp.arange(n_tiles, dtype=jnp.int32) * tm
    tile_expert = jnp.minimum(jnp.sum(tile_start[:, None] >= ends[None, :], axis=1), N_EXPERTS - 1).astype(jnp.int32)
    n_used = (ends[-1:] // tm).astype(jnp.int32)
    xs = _dispatch(h3, pos1, pos2, ends.astype(jnp.int32), n_tiles, tm)
    ys = _moe_ffn(xs, tile_expert, n_used, w_gate, w_up, w_down, tm)
    return _combine(ys, pos1, pos2, x3, rw, mod, seq)


def kernel(x, c, ada_w, ada_b, norm_g, ab_w_in, ab_conv_w, ab_conv_b, ab_cnorm_g, ab_cnorm_b, ab_q_g, ab_k_g,
           ab_w_out, ffn_w_gate, ffn_w_up, ffn_w_down, sc_w_in, sc_conv_w, sc_w_out, moe_router, moe_w_gate,
           moe_w_up, moe_w_down):
    batch, seq, d = x.shape
    assert d == D_MODEL and ada_w.shape[0] == 2 and seq % 1024 == 0
    tm = 512
    x2 = x.reshape(batch * seq, d)
    mod = _adaln(c, ada_w, ada_b)

    (a, q, k, v, qi, sm), bound = _in_proj0(x2, mod, norm_g[0, 0], ab_w_in[0], ab_q_g[0], ab_k_g[0], seq, tm)
    a = _conv_module(a.reshape(batch, seq, CONV_CH), ab_conv_w[0], ab_conv_b[0], ab_cnorm_g[0], ab_cnorm_b[0])
    att = _attention(q, k, v, qi, sm, bound, batch, seq)
    x2 = _mid0(a.reshape(batch * seq, CONV_CH), att, x2, mod, norm_g[0, 1], ab_w_out[0],
               ffn_w_gate[0], ffn_w_up[0], ffn_w_down[0], seq, tm)

    x3, h3, ri, rw, counts = _sc_layer(x2, mod, norm_g[1, 0], norm_g[1, 1], sc_w_in[0], sc_conv_w[0],
                                       sc_w_out[0], moe_router[0], seq, tm)
    out = _moe_layer(x3, h3, ri, rw, counts, mod, moe_w_gate[0], moe_w_up[0], moe_w_down[0], seq)
    return out.reshape(batch, seq, d)
```

```python
import functools

import jax
import jax.numpy as jnp
from jax import lax
from jax.experimental import pallas as pl
from jax.experimental.pallas import tpu as pltpu

D_MODEL = 1024
CHUNK = 64
CONV_CH = 512
CONV_WIDTH = 31
ATT_HEADS = 8
HEAD_DIM = 64
ATT_WIDTH = ATT_HEADS * HEAD_DIM
IDX_HEADS = 8
IDX_DIM = 64
TOPK_MAX = 256
SC_WIDTH = 1024
D_FF = 2816
N_EXPERTS = 8
EPS = 1e-6

LANES = 128
SUBLANES = 8
CONV_HALO = 32
SC_HALO = 8
NEG_BIG = -1e30
INT_MIN = -2 ** 31
I16_MIN = -2 ** 15
DMA_ISSUE_UNROLL = 8
LOG2_E = 1.4426950408889634
COUNT_ROWS = 64
MAX_CONST_SHIFT = 60.0
VMEM_LIMIT = 56 * 1024 * 1024

BF16 = jnp.bfloat16
F32 = jnp.float32


def _dot(a, b):
    return jnp.dot(a, b, preferred_element_type=F32)


def _dot_nt(a, b):
    return lax.dot_general(a, b, (((1,), (1,)), ((), ())), preferred_element_type=F32)


def _split_bf16(a):
    hi = a.astype(BF16)
    lo = (a - hi.astype(F32)).astype(BF16)
    return hi, lo


def _dot3(a, b):
    a_hi, a_lo = _split_bf16(a)
    b_hi, b_lo = _split_bf16(b)
    return _dot(a_hi, b_hi) + (_dot(a_hi, b_lo) + _dot(a_lo, b_hi))


def _silu(x):
    return x * jax.nn.sigmoid(x)


def _norm_mod(x, g, scale, shift):
    ms = jnp.mean(x * x, axis=-1, keepdims=True)
    return (x * lax.rsqrt(ms + EPS) * g) * (1.0 + scale) + shift


def _params(sem, vmem=VMEM_LIMIT):
    return pltpu.CompilerParams(dimension_semantics=sem, vmem_limit_bytes=vmem)


def _resident(shape):
    nd = len(shape)
    return pl.BlockSpec(shape, lambda *_: (0,) * nd, pipeline_mode=pl.Buffered(1))


def _ada_kernel(c_ref, w_ref, b_ref, o_ref):
    o_ref[...] = _dot3(_silu(c_ref[...]), w_ref[...]) + b_ref[...]


def _adaln(c, ada_w, ada_b):
    depth, d, n = ada_w.shape
    b = c.shape[0]
    tn = 1536
    mod = pl.pallas_call(
        _ada_kernel,
        grid=(depth, n // tn),
        in_specs=[
            pl.BlockSpec((b, d), lambda l, j: (0, 0)),
            pl.BlockSpec((None, d, tn), lambda l, j: (l, 0, j)),
            pl.BlockSpec((None, 1, tn), lambda l, j: (l, 0, j)),
        ],
        out_specs=pl.BlockSpec((None, b, tn), lambda l, j: (l, 0, j)),
        out_shape=jax.ShapeDtypeStruct((depth, b, n), F32),
        compiler_params=_params(("arbitrary", "arbitrary")),
        name="adaln",
    )(c, ada_w, ada_b.reshape(depth, 1, n))
    return mod.reshape(depth, b, 6, d)


def _mod_spec(layer, rows_per_batch_tiles):
    return pl.BlockSpec((None, None, 6, D_MODEL),
                        lambda i, *_: (layer, i // rows_per_batch_tiles, 0, 0))


def _in0_kernel(x_ref, mod_ref, g_ref, wm_ref, ws_ref, gm_ref, qg_ref, kg_ref,
                a_ref, q_ref, k_ref, v_ref, qi_ref, sm_ref):
    mod = mod_ref[...]
    h = _norm_mod(x_ref[...], g_ref[...], mod[1:2], mod[0:1]).astype(BF16)
    c = CONV_CH
    u = _dot(h, wm_ref[:, 0:2 * c])
    a_ref[...] = u[:, :c] * jax.nn.sigmoid(u[:, c:])

    def head_norm(t, g):
        ss = _dot((t * t).astype(BF16), gm_ref[...])
        return t * lax.rsqrt(ss * (1.0 / HEAD_DIM) + EPS) * g

    o = 2 * c
    w = ATT_WIDTH
    q = head_norm(_dot(h, wm_ref[:, o:o + w]), qg_ref[...]).astype(BF16)
    k_ref[...] = head_norm(_dot(h, wm_ref[:, o + w:o + 2 * w]), kg_ref[...]).astype(BF16)
    v = _dot(h, wm_ref[:, o + 2 * w:o + 3 * w]).astype(BF16)
    qi_ref[...] = _dot(h, wm_ref[:, o + 3 * w:o + 4 * w]).astype(BF16)
    low = lax.broadcasted_iota(jnp.int32, (q.shape[0], LANES), 1) < HEAD_DIM
    zero = jnp.zeros((), BF16)
    one = jnp.ones((), BF16)
    for p in range(ATT_HEADS // 2):
        ps = slice(p * LANES, (p + 1) * LANES)
        e = slice(2 * p * LANES, (2 * p + 1) * LANES)
        od = slice((2 * p + 1) * LANES, (2 * p + 2) * LANES)
        q_ref[:, e] = jnp.where(low, q[:, ps], zero)
        q_ref[:, od] = jnp.where(low, zero, q[:, ps])
        v_ref[:, e] = jnp.where(low, v[:, ps], one)
        v_ref[:, od] = jnp.where(low, one, v[:, ps])
    sm = _dot(h, ws_ref[...])
    lane = lax.broadcasted_iota(jnp.int32, sm.shape, 1)
    is_wi = (lane >= IDX_DIM) & (lane < IDX_DIM + IDX_HEADS)
    sm_ref[...] = jnp.where(is_wi, sm * ((IDX_DIM * IDX_HEADS) ** -0.5), sm)


def _in_proj0(x2, mod, g, w_in, q_g, k_g, seq, tm):
    m, d = x2.shape
    nmain = 2 * CONV_CH + 3 * ATT_WIDTH + IDX_HEADS * IDX_DIM
    nsmall = IDX_DIM + IDX_HEADS
    wm = w_in[:, :nmain].astype(BF16)
    ws = jnp.pad(w_in[:, nmain:nmain + nsmall], ((0, 0), (0, LANES - nsmall))).astype(BF16)
    hid = jnp.arange(ATT_WIDTH) // HEAD_DIM
    gm = (hid[:, None] == hid[None, :]).astype(BF16)
    qg = (jnp.tile(q_g, ATT_HEADS) * (HEAD_DIM ** -0.5 * LOG2_E))[None]
    kg = jnp.tile(k_g, ATT_HEADS)[None]
    row = lambda n: pl.BlockSpec((tm, n), lambda i: (i, 0))
    bound = (jnp.max(jnp.abs(qg)) * jnp.max(jnp.abs(kg)) * (HEAD_DIM * 1.01)).reshape(1)
    outs = pl.pallas_call(
        _in0_kernel,
        grid=(m // tm,),
        in_specs=[row(d), _mod_spec(0, seq // tm), _resident((1, d)), _resident(wm.shape),
                  _resident(ws.shape), _resident(gm.shape), _resident(qg.shape), _resident(kg.shape)],
        out_specs=[row(CONV_CH), row(2 * ATT_WIDTH), row(ATT_WIDTH), row(2 * ATT_WIDTH), row(ATT_WIDTH), row(LANES)],
        out_shape=[jax.ShapeDtypeStruct((m, CONV_CH), F32),
                   jax.ShapeDtypeStruct((m, 2 * ATT_WIDTH), BF16), jax.ShapeDtypeStruct((m, ATT_WIDTH), BF16),
                   jax.ShapeDtypeStruct((m, 2 * ATT_WIDTH), BF16), jax.ShapeDtypeStruct((m, ATT_WIDTH), BF16),
                   jax.ShapeDtypeStruct((m, LANES), F32)],
        compiler_params=_params(("parallel",)),
        name="in_proj0",
    )(x2, mod, g[None], wm, ws, gm, qg, kg)
    return outs, bound


def _conv_kernel(prev_ref, cur_ref, w_ref, b_ref, g_ref, beta_ref, o_ref, buf, shifted, *, ts, rc):
    j = pl.program_id(1)
    buf[0:CONV_HALO, :] = jnp.where(j > 0, prev_ref[...], 0.0)
    buf[CONV_HALO:, :] = cur_ref[...]
    w = w_ref[...]
    lead = CONV_HALO - (CONV_WIDTH - 1)
    span = ts + CONV_HALO - SUBLANES
    for b in range(1, SUBLANES):
        shifted[b - 1] = buf[b:b + span, :]
    for r in range(ts // rc):
        acc = jnp.broadcast_to(b_ref[...], (rc, CONV_CH))
        for k in range(CONV_WIDTH):
            a, b = divmod(k + lead, SUBLANES)
            s = r * rc + SUBLANES * a
            src = buf[s:s + rc, :] if b == 0 else shifted[b - 1, s:s + rc, :]
            acc = acc + w[k:k + 1, :] * src
        mu = jnp.mean(acc, axis=-1, keepdims=True)
        dlt = acc - mu
        var = jnp.mean(dlt * dlt, axis=-1, keepdims=True)
        y = dlt * lax.rsqrt(var + EPS) * g_ref[...] + beta_ref[...]
        o_ref[r * rc:(r + 1) * rc, :] = _silu(y).astype(BF16)


def _conv_module(a3, conv_w, conv_b, cn_g, cn_b, ts=512, rc=64):
    b, s, c = a3.shape
    wpad = jnp.pad(conv_w, ((0, CONV_HALO - CONV_WIDTH), (0, 0)))
    hb = ts // CONV_HALO
    return pl.pallas_call(
        functools.partial(_conv_kernel, ts=ts, rc=rc),
        grid=(b, s // ts),
        in_specs=[
            pl.BlockSpec((None, CONV_HALO, c), lambda bi, j: (bi, jnp.maximum(j * hb - 1, 0), 0)),
            pl.BlockSpec((None, ts, c), lambda bi, j: (bi, j, 0)),
            _resident(wpad.shape), _resident((1, c)), _resident((1, c)), _resident((1, c)),
        ],
        out_specs=pl.BlockSpec((None, ts, c), lambda bi, j: (bi, j, 0)),
        out_shape=jax.ShapeDtypeStruct((b, s, c), BF16),
        scratch_shapes=[pltpu.VMEM((ts + CONV_HALO, c), F32),
                        pltpu.VMEM((SUBLANES - 1, ts + CONV_HALO - SUBLANES, c), F32)],
        compiler_params=_params(("parallel", "parallel")),
        name="conv_module",
    )(a3, a3, wpad, conv_b[None], cn_g[None], cn_b[None])


def _attn_kernel(bnd_ref, q_ref, qi_ref, smq_ref, k_ref, v_ref, smk_ref, o_ref,
                 key_scr, hi_scr, lo_scr, wb_scr, st_scr, s16_scr, need_scr, m_scr, acc_scr,
                 *, tq, tk, seq, topk):
    j = pl.program_id(1)
    q0 = j * tq
    nkt = (q0 + tq + tk - 1) // tk
    imin = jnp.int32(INT_MIN)
    row = lax.broadcasted_iota(jnp.int32, (tq, 1), 0) + q0
    limit = (row // CHUNK + 1) * CHUNK
    lane_pos = lax.broadcasted_iota(jnp.int32, (tq, tk), 1)

    wi = smq_ref[:, IDX_DIM:IDX_DIM + IDX_HEADS]
    for h in range(IDX_HEADS):
        wb_scr[h] = jnp.broadcast_to(wi[:, h:h + 1], (tq, LANES))

    def score_tile(kt, _):
        k0 = pl.multiple_of(kt * tk, tk)
        ki = smk_ref[pl.ds(k0, tk), 0:IDX_DIM].astype(BF16)
        acc = jnp.zeros((tq, tk), F32)
        for h in range(IDX_HEADS):
            d = _dot_nt(qi_ref[:, h * IDX_DIM:(h + 1) * IDX_DIM], ki)
            acc = acc + jnp.tile(wb_scr[h], (1, tk // LANES)) * jnp.maximum(d, 0.0)
        bits = pltpu.bitcast(acc, jnp.int32)
        skey = bits ^ ((bits >> 31) & jnp.int32(0x7FFFFFFF))
        skey = jnp.where(lane_pos + k0 < limit, skey, imin)
        key_scr[kt] = skey
        hi_scr[kt] = (skey >> 16).astype(jnp.int16)
        lo_scr[kt] = ((skey & jnp.int32(0xFFFF)) + I16_MIN).astype(jnp.int16)
        return 0

    lax.fori_loop(0, nkt, score_tile, 0)

    lane128 = lax.broadcasted_iota(jnp.int32, (tq, LANES), 1)

    kf = jnp.float32(topk)
    rows = COUNT_ROWS
    row_chunks = [slice(r * rows, (r + 1) * rows) for r in range(tq // rows)]
    lane_r = lax.broadcasted_iota(jnp.int32, (rows, LANES), 1)
    ones = jnp.ones((LANES, LANES), BF16)
    t_scr, cand_scr, v_scr, thi_scr = st_scr.at[0], st_scr.at[1], st_scr.at[2], st_scr.at[3]
    cand16_scr, thr16_scr = s16_scr.at[0], s16_scr.at[1]
    one16 = jnp.ones((), jnp.int16)
    zero16 = jnp.zeros((), jnp.int16)

    def count16(pred, src, *state):
        def body(kt, parts):
            out = []
            for rs, part in zip(row_chunks, parts):
                ops = [r[rs, :] for r in state]
                for c in range(tk // LANES):
                    x = src[kt, rs, c * LANES:(c + 1) * LANES]
                    part = part + jnp.where(pred(x, *ops), one16, zero16)
                out.append(part)
            return tuple(out)
        parts = lax.fori_loop(0, nkt, body, tuple(jnp.zeros((rows, LANES), jnp.int16) for _ in row_chunks))
        return _dot(jnp.concatenate(parts, axis=0).astype(F32).astype(BF16), ones)

    def digit_search(src):
        c0 = count16(lambda x: x >= zero16, src)
        t_scr[...] = jnp.where(c0 >= need_scr[...], jnp.int32(0), jnp.int32(I16_MIN))

        def bit_step(i, _):
            cand = t_scr[...] | (jnp.int32(1) << (14 - i))
            cand_scr[...] = cand
            cand16_scr[...] = cand.astype(jnp.int16)
            c = count16(lambda x, cd: x >= cd, src, cand16_scr)
            t_scr[...] = jnp.where(c >= need_scr[...], cand_scr[...], t_scr[...])
            return 0

        lax.fori_loop(0, 15, bit_step, 0)

    def count(pred, *state):
        def body(kt, parts):
            out = []
            for rs, part in zip(row_chunks, parts):
                ops = [r[rs, :] for r in state]
                for c in range(tk // LANES):
                    sk = key_scr[kt, rs, c * LANES:(c + 1) * LANES]
                    part = part + jnp.where(pred(sk, lane_r + (kt * tk + c * LANES), *ops), 1.0, 0.0)
                out.append(part)
            return tuple(out)
        parts = lax.fori_loop(0, nkt, body, tuple(jnp.zeros((rows, LANES), F32) for _ in row_chunks))
        return _dot(jnp.concatenate(parts, axis=0).astype(BF16), ones)

    @pl.when(q0 + tq > topk)
    def _():
        need_scr[...] = jnp.full((tq, LANES), kf, F32)
        digit_search(hi_scr)
        thi_scr[...] = t_scr[...]
        thr16_scr[...] = t_scr[...].astype(jnp.int16)
        above = count16(lambda x, th: x > th, hi_scr, thr16_scr)
        need_scr[...] = kf - above

        def bucket_tile(kt, _):
            for rs in row_chunks:
                th = thr16_scr[rs, :]
                for c in range(tk // LANES):
                    cs = slice(c * LANES, (c + 1) * LANES)
                    lo_scr[kt, rs, cs] = jnp.where(hi_scr[kt, rs, cs] == th, lo_scr[kt, rs, cs],
                                                   jnp.int16(I16_MIN))
            return 0

        lax.fori_loop(0, nkt, bucket_tile, 0)
        digit_search(lo_scr)
        thr16_scr[...] = t_scr[...].astype(jnp.int16)
        c_gt = above + count16(lambda x, th: x > th, lo_scr, thr16_scr)
        c_ge = above + count16(lambda x, th: x >= th, lo_scr, thr16_scr)
        t_scr[...] = (thi_scr[...] << 16) | ((t_scr[...] - I16_MIN) & jnp.int32(0xFFFF))
        need = kf - c_gt
        has_tie = jnp.max(c_ge) > kf

        @pl.when(has_tie)
        def _():
            v_scr[...] = jnp.zeros((tq, LANES), jnp.int32)

            def idx_step(i, _):
                cand_scr[...] = v_scr[...] | (jnp.int32(1) << (seq.bit_length() - 2 - i))
                f = count(lambda sk, pos, tt, cd: (sk == tt) & (pos < cd), t_scr, cand_scr)
                v_scr[...] = jnp.where(f < need, cand_scr[...], v_scr[...])
                return 0

            lax.fori_loop(0, seq.bit_length() - 1, idx_step, 0)

        @pl.when(jnp.logical_not(has_tie))
        def _():
            v_scr[...] = jnp.full((tq, LANES), seq - 1, jnp.int32)

    @pl.when(q0 + tq <= topk)
    def _():
        t_scr[...] = jnp.full((tq, LANES), INT_MIN, jnp.int32)
        v_scr[...] = jnp.full((tq, LANES), seq - 1, jnp.int32)

    bound = bnd_ref[0]
    const_shift = bound <= MAX_CONST_SHIFT
    shift = jnp.where(const_shift, bound, 0.0)

    def bias_tile(kt, _):
        for rs in row_chunks:
            t = t_scr[rs, :]
            v = v_scr[rs, :]
            for c in range(tk // LANES):
                cs = slice(c * LANES, (c + 1) * LANES)
                sk = key_scr[kt, rs, cs]
                pos = lane_r + (kt * tk + c * LANES)
                sel = ((sk > t) | ((sk == t) & (pos <= v))) & (sk > imin)
                key_scr[kt, rs, cs] = pltpu.bitcast(jnp.where(sel, -shift, NEG_BIG).astype(F32), jnp.int32)
        return 0

    lax.fori_loop(0, nkt, bias_tile, 0)

    acc_scr[...] = jnp.zeros(acc_scr.shape, F32)

    def logits(kt, h):
        k0 = pl.multiple_of(kt * tk, tk)
        ps = slice((h // 2) * LANES, (h // 2 + 1) * LANES)
        return (_dot_nt(q_ref[:, h * LANES:(h + 1) * LANES], k_ref[pl.ds(k0, tk), ps])
                + pltpu.bitcast(key_scr[kt], F32))

    def values(kt, h):
        return v_ref[pl.ds(pl.multiple_of(kt * tk, tk), tk), h * LANES:(h + 1) * LANES]

    def att_tile_const(kt, _):
        for h in range(ATT_HEADS):
            p = jnp.exp2(logits(kt, h)).astype(BF16)
            acc_scr[h] = acc_scr[h] + _dot(p, values(kt, h))
        return 0

    def att_tile_online(kt, _):
        for h in range(ATT_HEADS):
            s = logits(kt, h)
            m_old = m_scr[h]
            m_new = jnp.maximum(m_old, jnp.max(s, axis=1, keepdims=True))
            alpha = jnp.exp2(m_old - m_new)
            p = jnp.concatenate(
                [jnp.exp2(s[:, c * LANES:(c + 1) * LANES] - m_new) for c in range(tk // LANES)], axis=1)
            acc_scr[h] = alpha * acc_scr[h] + _dot(p.astype(BF16), values(kt, h))
            m_scr[h] = m_new
        return 0

    @pl.when(const_shift)
    def _():
        lax.fori_loop(0, nkt, att_tile_const, 0)

    @pl.when(jnp.logical_not(const_shift))
    def _():
        m_scr[...] = jnp.full(m_scr.shape, NEG_BIG, F32)
        lax.fori_loop(0, nkt, att_tile_online, 0)

    low = lane128 < HEAD_DIM
    for p in range(ATT_HEADS // 2):
        ae = acc_scr[2 * p]
        ao = acc_scr[2 * p + 1]
        oe = ae / pltpu.roll(ae, HEAD_DIM, 1)
        oo = ao / pltpu.roll(ao, HEAD_DIM, 1)
        o_ref[:, p * LANES:(p + 1) * LANES] = jnp.where(low, oe, oo).astype(BF16)


def _attention(q, k, v, qi, sm, bound, batch, seq, tq=512, tk=512):
    m = q.shape[0]
    topk = min(TOPK_MAX, seq // 4)
    nq = seq // tq
    qrow = lambda n: pl.BlockSpec((tq, n), lambda b, j: (b * nq + j, 0))
    krow = lambda n: pl.BlockSpec((seq, n), lambda b, j: (b, 0), pipeline_mode=pl.Buffered(1))
    return pl.pallas_call(
        functools.partial(_attn_kernel, tq=tq, tk=tk, seq=seq, topk=topk),
        grid=(batch, nq),
        in_specs=[pl.BlockSpec(memory_space=pltpu.SMEM),
                  qrow(2 * ATT_WIDTH), qrow(ATT_WIDTH), qrow(LANES),
                  krow(ATT_WIDTH), krow(2 * ATT_WIDTH), krow(LANES)],
        out_specs=qrow(ATT_WIDTH),
        out_shape=jax.ShapeDtypeStruct((m, ATT_WIDTH), BF16),
        scratch_shapes=[pltpu.VMEM((seq // tk, tq, tk), jnp.int32),
                        pltpu.VMEM((seq // tk, tq, tk), jnp.int16),
                        pltpu.VMEM((seq // tk, tq, tk), jnp.int16),
                        pltpu.VMEM((IDX_HEADS, tq, LANES), F32),
                        pltpu.VMEM((4, tq, LANES), jnp.int32),
                        pltpu.VMEM((2, tq, LANES), jnp.int16),
                        pltpu.VMEM((tq, LANES), F32),
                        pltpu.VMEM((ATT_HEADS, tq, LANES), F32),
                        pltpu.VMEM((ATT_HEADS, tq, LANES), F32)],
        compiler_params=_params(("parallel", "arbitrary")),
        name="dsa_attention",
    )(bound, q, qi, sm, k, v, sm)


def _mid_kernel(a_ref, att_ref, x_ref, mod_ref, g_ref, wa_ref, wb_ref, wg_ref, wu_ref, wd_ref, o_ref, *, fc):
    mod = mod_ref[...]
    mix = _dot(a_ref[...], wa_ref[...]) + _dot(att_ref[...], wb_ref[...])
    x1 = x_ref[...] + mod[2:3] * mix
    h = _norm_mod(x1, g_ref[...], mod[4:5], mod[3:4]).astype(BF16)
    y = jnp.zeros(x1.shape, F32)
    for c in range(D_FF // fc):
        cs = slice(c * fc, (c + 1) * fc)
        act = (_silu(_dot(h, wg_ref[:, cs])) * _dot(h, wu_ref[:, cs])).astype(BF16)
        y = y + _dot(act, wd_ref[cs, :])
    o_ref[...] = x1 + mod[5:6] * y


def _mid0(a, att, x2, mod, g, w_out, w_gate, w_up, w_down, seq, tm):
    m, d = x2.shape
    wa = w_out[:CONV_CH].astype(BF16)
    wb = w_out[CONV_CH:].astype(BF16)
    wg, wu, wd = w_gate.astype(BF16), w_up.astype(BF16), w_down.astype(BF16)
    row = lambda n: pl.BlockSpec((tm, n), lambda i: (i, 0))
    return pl.pallas_call(
        functools.partial(_mid_kernel, fc=D_FF // 2),
        grid=(m // tm,),
        in_specs=[row(CONV_CH), row(ATT_WIDTH), row(d), _mod_spec(0, seq // tm), _resident((1, d)),
                  _resident(wa.shape), _resident(wb.shape), _resident(wg.shape), _resident(wu.shape),
                  _resident(wd.shape)],
        out_specs=row(d),
        out_shape=jax.ShapeDtypeStruct((m, d), F32),
        compiler_params=_params(("parallel",)),
        name="outproj_ffn0",
    )(a, att, x2, mod, g[None], wa, wb, wg, wu, wd)


def _sc_kernel(x_ref, mod_ref, g1_ref, g2_ref, win_ref, cw_ref, wout_ref, wr_ref, tri_ref,
               x3_ref, h3_ref, ri_ref, rw_ref, cnt_ref, buf, cnt_scr, *, tm, tiles_per_seq):
    i = pl.program_id(0)
    mod = mod_ref[...]
    w = SC_WIDTH

    @pl.when(i == 0)
    def _():
        cnt_scr[...] = jnp.zeros_like(cnt_scr)

    @pl.when(i % tiles_per_seq == 0)
    def _():
        buf[0:SC_HALO, :] = jnp.zeros((SC_HALO, w), F32)

    x = x_ref[...]
    h = _norm_mod(x, g1_ref[...], mod[1:2], mod[0:1]).astype(BF16)
    bg = _dot(h, win_ref[:, 0:w])
    buf[SC_HALO:, :] = _dot(h, win_ref[:, w:2 * w]) * _dot(h, win_ref[:, 2 * w:3 * w])
    cw = cw_ref[...]
    conv = (cw[0:1] * buf[SC_HALO - 2:SC_HALO - 2 + tm, :]
            + cw[1:2] * buf[SC_HALO - 1:SC_HALO - 1 + tm, :]
            + cw[2:3] * buf[SC_HALO:SC_HALO + tm, :])
    buf[0:SC_HALO, :] = buf[tm:tm + SC_HALO, :]
    x3 = x + mod[2:3] * _dot((bg * conv).astype(BF16), wout_ref[...])
    x3_ref[...] = x3
    h3 = _norm_mod(x3, g2_ref[...], mod[4:5], mod[3:4])
    h3_ref[...] = h3

    logits = _dot3(h3, wr_ref[...])
    lane = lax.broadcasted_iota(jnp.int32, logits.shape, 1)
    lane_f = lane.astype(F32)
    logits = jnp.where(lane < N_EXPERTS, logits, -jnp.inf)
    m1 = jnp.max(logits, axis=1, keepdims=True)
    i1 = jnp.min(jnp.where(logits == m1, lane_f, float(LANES)), axis=1, keepdims=True)
    rest = jnp.where(lane_f == i1, -jnp.inf, logits)
    m2 = jnp.max(rest, axis=1, keepdims=True)
    i2 = jnp.min(jnp.where(rest == m2, lane_f, float(LANES)), axis=1, keepdims=True)
    e = jnp.exp(m2 - m1)
    w1 = 1.0 / (1.0 + e)
    w2 = e / (1.0 + e)

    oh1 = lane_f == i1
    oh2 = lane_f == i2
    oh = jnp.where(oh1 | oh2, 1.0, 0.0)
    before = _dot(tri_ref[...], oh.astype(BF16)) + cnt_scr[...]
    r1 = jnp.sum(jnp.where(oh1, before, 0.0), axis=1, keepdims=True)
    r2 = jnp.sum(jnp.where(oh2, before, 0.0), axis=1, keepdims=True)
    cnt_scr[...] = cnt_scr[...] + jnp.sum(oh, axis=0, keepdims=True)
    cnt_ref[...] = jnp.broadcast_to(cnt_scr[...], cnt_ref.shape)

    r1i = r1.astype(jnp.int32)
    r2i = r2.astype(jnp.int32)
    ri_ref[...] = jnp.where(lane == 0, i1.astype(jnp.int32),
                            jnp.where(lane == 1, i2.astype(jnp.int32), jnp.where(lane == 2, r1i, r2i)))
    rw_ref[...] = jnp.where(lane == 0, w1, w2)


def _sc_layer(x2, mod, g1, g2, w_in, conv_w, w_out, w_router, seq, tm):
    m, d = x2.shape
    win = w_in.astype(BF16)
    wout = w_out.astype(BF16)
    wr = jnp.pad(w_router, ((0, 0), (0, LANES - N_EXPERTS)))
    cw = jnp.pad(conv_w, ((0, 8 - conv_w.shape[0]), (0, 0)))
    idx = jnp.arange(tm)
    tri = (idx[None, :] < idx[:, None]).astype(BF16)
    row = lambda n: pl.BlockSpec((tm, n), lambda i: (i, 0))
    return pl.pallas_call(
        functools.partial(_sc_kernel, tm=tm, tiles_per_seq=seq // tm),
        grid=(m // tm,),
        in_specs=[row(d), _mod_spec(1, seq // tm), _resident((1, d)), _resident((1, d)), _resident(win.shape),
                  _resident(cw.shape), _resident(wout.shape), _resident(wr.shape), _resident(tri.shape)],
        out_specs=[row(d), row(d), row(LANES), row(LANES), pl.BlockSpec((8, LANES), lambda i: (0, 0))],
        out_shape=[jax.ShapeDtypeStruct((m, d), F32), jax.ShapeDtypeStruct((m, d), F32),
                   jax.ShapeDtypeStruct((m, LANES), jnp.int32), jax.ShapeDtypeStruct((m, LANES), F32),
                   jax.ShapeDtypeStruct((8, LANES), F32)],
        scratch_shapes=[pltpu.VMEM((tm + SC_HALO, SC_WIDTH), F32), pltpu.VMEM((1, LANES), F32)],
        compiler_params=_params(("arbitrary",)),
        name="shortconv_router",
    )(x2, mod, g1[None], g2[None], win, cw, wout, wr, tri)


def _dispatch_kernel(ends_ref, p1_ref, p2_ref, h_ref, xs_ref, zbuf, sems, zsem, *, td, tm, n_tiles):
    @pl.when(pl.program_id(0) == 0)
    def _():
        zbuf[...] = jnp.zeros(zbuf.shape, F32)
        n_used = ends_ref[N_EXPERTS - 1] // tm
        starts, conds = [], []
        for e in range(N_EXPERTS):
            prev = ends_ref[e - 1] if e else 0
            starts.append(ends_ref[e] - tm)
            conds.append(ends_ref[e] > prev)
        for u in range(N_EXPERTS):
            starts.append((n_used + u) * tm)
            conds.append(n_used + u < n_tiles)
        zero_copy = lambda s: pltpu.make_async_copy(zbuf, xs_ref.at[pl.ds(pl.multiple_of(s, tm), tm)], zsem)
        for s, c in zip(starts, conds):
            pl.when(c)(lambda s=s: zero_copy(s).start())
        for s, c in zip(starts, conds):
            pl.when(c)(lambda s=s: zero_copy(s).wait())

    def issue(t, _):
        src = h_ref.at[pl.ds(t, 1)]
        pltpu.make_async_copy(src, xs_ref.at[pl.ds(p1_ref[t], 1)], sems.at[0]).start()
        pltpu.make_async_copy(src, xs_ref.at[pl.ds(p2_ref[t], 1)], sems.at[1]).start()
        return 0

    lax.fori_loop(0, td, issue, 0, unroll=DMA_ISSUE_UNROLL)
    pltpu.make_async_copy(h_ref, xs_ref.at[pl.ds(0, td)], sems.at[0]).wait()
    pltpu.make_async_copy(h_ref, xs_ref.at[pl.ds(0, td)], sems.at[1]).wait()


def _dispatch(h3, pos1, pos2, ends, n_tiles, tm, td=1024):
    m, d = h3.shape
    smem = lambda: pl.BlockSpec((td,), lambda i, ends: (i,), memory_space=pltpu.SMEM)
    return pl.pallas_call(
        functools.partial(_dispatch_kernel, td=td, tm=tm, n_tiles=n_tiles),
        grid_spec=pltpu.PrefetchScalarGridSpec(
            num_scalar_prefetch=1,
            grid=(m // td,),
            in_specs=[smem(), smem(), pl.BlockSpec((td, d), lambda i, ends: (i, 0))],
            out_specs=pl.BlockSpec(memory_space=pl.ANY),
            scratch_shapes=[pltpu.VMEM((tm, d), F32), pltpu.SemaphoreType.DMA((2,)), pltpu.SemaphoreType.DMA(())],
        ),
        out_shape=jax.ShapeDtypeStruct((n_tiles * tm, d), F32),
        compiler_params=_params(("arbitrary",)),
        name="moe_dispatch",
    )(ends, pos1, pos2, h3)


def _moe_kernel(te_ref, nu_ref, xs_ref, wg_ref, wu_ref, wd_ref, ys_ref, *, fc):
    del te_ref
    i = pl.program_id(0)

    @pl.when(i < nu_ref[0])
    def _():
        x = xs_ref[...].astype(BF16)
        y = jnp.zeros(ys_ref.shape, F32)
        for c in range(D_FF // fc):
            cs = slice(c * fc, (c + 1) * fc)
            act = (_silu(_dot(x, wg_ref[:, cs])) * _dot(x, wu_ref[:, cs])).astype(BF16)
            y = y + _dot(act, wd_ref[cs, :])
        ys_ref[...] = y

    @pl.when(i >= nu_ref[0])
    def _():
        ys_ref[...] = jnp.zeros(ys_ref.shape, F32)


def _moe_ffn(xs, tile_expert, n_used, w_gate, w_up, w_down, tm):
    rows, d = xs.shape
    wg, wu, wd = w_gate.astype(BF16), w_up.astype(BF16), w_down.astype(BF16)
    wspec = lambda shape: pl.BlockSpec((None,) + shape, lambda i, te, nu: (te[i], 0, 0),
                                       pipeline_mode=pl.Buffered(1))
    return pl.pallas_call(
        functools.partial(_moe_kernel, fc=D_FF // 2),
        grid_spec=pltpu.PrefetchScalarGridSpec(
            num_scalar_prefetch=2,
            grid=(rows // tm,),
            in_specs=[pl.BlockSpec((tm, d), lambda i, te, nu: (i, 0)),
                      wspec((d, D_FF)), wspec((d, D_FF)), wspec((D_FF, d))],
            out_specs=pl.BlockSpec((tm, d), lambda i, te, nu: (i, 0)),
        ),
        out_shape=jax.ShapeDtypeStruct((rows, d), F32),
        compiler_params=_params(("arbitrary",)),
        name="moe_experts",
    )(tile_expert, n_used, xs, wg, wu, wd)


def _combine_kernel(p1_ref, p2_ref, ys_ref, x_ref, rw_ref, mod_ref, o_ref, buf, sems, *, tc):
    def issue(t, _):
        pltpu.make_async_copy(ys_ref.at[pl.ds(p1_ref[t], 1)], buf.at[0, pl.ds(t, 1)], sems.at[0]).start()
        pltpu.make_async_copy(ys_ref.at[pl.ds(p2_ref[t], 1)], buf.at[1, pl.ds(t, 1)], sems.at[1]).start()
        return 0

    lax.fori_loop(0, tc, issue, 0, unroll=DMA_ISSUE_UNROLL)
    pltpu.make_async_copy(ys_ref.at[pl.ds(0, tc)], buf.at[0], sems.at[0]).wait()
    pltpu.make_async_copy(ys_ref.at[pl.ds(0, tc)], buf.at[1], sems.at[1]).wait()
    rw = rw_ref[...]
    moe = rw[:, 0:1] * buf[0] + rw[:, 1:2] * buf[1]
    o_ref[...] = x_ref[...] + mod_ref[...][5:6] * moe


def _combine(ys, pos1, pos2, x3, rw, mod, seq, tc=1024):
    m, d = x3.shape
    smem = lambda: pl.BlockSpec((tc,), lambda i: (i,), memory_space=pltpu.SMEM)
    row = lambda n: pl.BlockSpec((tc, n), lambda i: (i, 0))
    return pl.pallas_call(
        functools.partial(_combine_kernel, tc=tc),
        grid=(m // tc,),
        in_specs=[smem(), smem(), pl.BlockSpec(memory_space=pl.ANY), row(d), row(LANES), _mod_spec(1, seq // tc)],
        out_specs=row(d),
        out_shape=jax.ShapeDtypeStruct((m, d), F32),
        scratch_shapes=[pltpu.VMEM((2, tc, d), F32), pltpu.SemaphoreType.DMA((2,))],
        compiler_params=_params(("arbitrary",)),
        name="moe_combine",
    )(pos1, pos2, ys, x3, rw, mod)


def _moe_layer(x3, h3, ri, rw, counts, mod, w_gate, w_up, w_down, seq, tm=512):
    m, d = x3.shape
    n_tiles = (2 * m) // tm + N_EXPERTS
    cnt = counts[0, :N_EXPERTS].astype(jnp.int32)
    padded = ((cnt + tm - 1) // tm) * tm
    ends = jnp.cumsum(padded)
    offs = ends - padded
    pos1 = offs[ri[:, 0]] + ri[:, 2]
    pos2 = offs[ri[:, 1]] + ri[:, 3]
    tile_start = jnp.arange(n_tiles, dtype=jnp.int32) * tm
    tile_expert = jnp.minimum(jnp.sum(tile_start[:, None] >= ends[None, :], axis=1), N_EXPERTS - 1).astype(jnp.int32)
    n_used = (ends[-1:] // tm).astype(jnp.int32)
    xs = _dispatch(h3, pos1, pos2, ends.astype(jnp.int32), n_tiles, tm)
    ys = _moe_ffn(xs, tile_expert, n_used, w_gate, w_up, w_down, tm)
    return _combine(ys, pos1, pos2, x3, rw, mod, seq)


def kernel(x, c, ada_w, ada_b, norm_g, ab_w_in, ab_conv_w, ab_conv_b, ab_cnorm_g, ab_cnorm_b, ab_q_g, ab_k_g,
           ab_w_out, ffn_w_gate, ffn_w_up, ffn_w_down, sc_w_in, sc_conv_w, sc_w_out, moe_router, moe_w_gate,
           moe_w_up, moe_w_down):
    batch, seq, d = x.shape
    assert d == D_MODEL and ada_w.shape[0] == 2 and seq % 1024 == 0
    tm = 512
    x2 = x.reshape(batch * seq, d)
    mod = _adaln(c, ada_w, ada_b)

    (a, q, k, v, qi, sm), bound = _in_proj0(x2, mod, norm_g[0, 0], ab_w_in[0], ab_q_g[0], ab_k_g[0], seq, tm)
    a = _conv_module(a.reshape(batch, seq, CONV_CH), ab_conv_w[0], ab_conv_b[0], ab_cnorm_g[0], ab_cnorm_b[0])
    att = _attention(q, k, v, qi, sm, bound, batch, seq)
    x2 = _mid0(a.reshape(batch * seq, CONV_CH), att, x2, mod, norm_g[0, 1], ab_w_out[0],
               ffn_w_gate[0], ffn_w_up[0], ffn_w_down[0], seq, tm)

    x3, h3, ri, rw, counts = _sc_layer(x2, mod, norm_g[1, 0], norm_g[1, 1], sc_w_in[0], sc_conv_w[0],
                                       sc_w_out[0], moe_router[0], seq, tm)
    out = _moe_layer(x3, h3, ri, rw, counts, mod, moe_w_gate[0], moe_w_up[0], moe_w_down[0], seq)
    return out.reshape(batch, seq, d)
```
